```python
import jax, jax.numpy as jnp
from jax import lax
import numpy as np

D_MODEL = 4096
BATCH = 32
SEQ = 256
DEPTH = 2
DEC_BATCH = 8
DEC_SEQ = 4096
PAST_LEN = 256

GRID_W = 64
HEAD_DIM = 128
N_HEADS = (3 * D_MODEL // 4) // HEAD_DIM
N_KV_HEADS = N_HEADS // 3
Q_WIDTH = N_HEADS * HEAD_DIM
KV_WIDTH = N_KV_HEADS * HEAD_DIM
A_GROUPS = 4
A_WIDTH = D_MODEL - Q_WIDTH
A_GROUP_DIM = A_WIDTH // A_GROUPS
EVEN_IN = A_WIDTH + Q_WIDTH + 2 * KV_WIDTH
Q_BLOCK = 128
ROPE_BASE = 10000.0
ROPE_FREQS = HEAD_DIM // 4
CHUNK = 128
C_WIDTH = D_MODEL
C_GROUPS = 8
C_GROUP_DIM = C_WIDTH // C_GROUPS
FFN_DIM = 11008
CONV_WIDTH = 3
N_MOD = 6
N_EVEN = (DEPTH + 1) // 2
N_ODD = DEPTH // 2
DEEPNORM_ALPHA = (2 * DEPTH) ** 0.25
DEEPNORM_BETA = (8 * DEPTH) ** -0.25
LN_EPS = 1e-6
MOD_SCALE = 0.5

kernel_name = 'hybrid_flow_fnet_gqa_gmlp_convffn_step'


def _standardize(x):
    xf = x.astype(jnp.float32)
    xc = xf - jnp.mean(xf, axis=-1, keepdims=True)
    return xc * lax.rsqrt(jnp.mean(xc * xc, axis=-1, keepdims=True) + LN_EPS)


def layer_norm(x, g, b):
    return (_standardize(x) * g + b).astype(x.dtype)


def rms_norm(x, g):
    xf = x.astype(jnp.float32)
    return (xf * lax.rsqrt(jnp.mean(xf * xf, axis=-1, keepdims=True) + LN_EPS) * g).astype(x.dtype)


def adaln(cond, w, b):
    m = (jax.nn.silu(cond) @ w + b).reshape(cond.shape[0], N_MOD, 1, D_MODEL)
    return [m[:, j] for j in range(N_MOD)]


def modulate(x, shift, scale):
    return (_standardize(x) * (1.0 + scale) + shift).astype(x.dtype)


def residual_norm(x, y, gate, g, b):
    return layer_norm(DEEPNORM_ALPHA * x + gate * y, g, b)


def axial_rope_tables(n_tokens):
    rows = n_tokens // GRID_W
    row = jnp.repeat(jnp.arange(rows), GRID_W).astype(jnp.float32)
    col = (jnp.arange(n_tokens) % GRID_W).astype(jnp.float32)
    inv = ROPE_BASE ** (-jnp.arange(ROPE_FREQS, dtype=jnp.float32) / ROPE_FREQS)
    ang = jnp.stack([row[:, None] * inv, col[:, None] * inv], axis=1)
    return jnp.cos(ang), jnp.sin(ang)


def apply_axial_rope(x, cos, sin):
    B, L, H, _ = x.shape
    xr = x.reshape(B, L, H, 2, 2, ROPE_FREQS).astype(jnp.float32)
    x1, x2 = xr[..., 0, :], xr[..., 1, :]
    c = cos[None, :, None]
    s = sin[None, :, None]
    out = jnp.stack([x1 * c - x2 * s, x1 * s + x2 * c], axis=-2)
    return out.reshape(B, L, H, HEAD_DIM).astype(x.dtype)


def gqa_attend(q, k, v):
    B, Lq, H, Dh = q.shape
    kvh = k.shape[2]
    qg = q.reshape(B, Lq, kvh, H // kvh, Dh)
    s = jnp.einsum('bqkgd,bskd->bkgqs', qg, k).astype(jnp.float32) * (HEAD_DIM ** -0.5)
    p = jax.nn.softmax(s, axis=-1).astype(v.dtype)
    o = jnp.einsum('bkgqs,bskd->bqkgd', p, v)
    return o.reshape(B, Lq, H, Dh)


def blocked_attend(q, k, v):
    B, Lq, H, Dh = q.shape
    nb = Lq // Q_BLOCK
    qb = jnp.moveaxis(q.reshape(B, nb, Q_BLOCK, H, Dh), 1, 0)
    ob = lax.map(lambda qi: gqa_attend(qi, k, v), qb)
    return jnp.moveaxis(ob, 0, 1).reshape(B, Lq, H, Dh)


def even_project(h, w_in, q_g, k_g):
    B, L, _ = h.shape
    z = h @ w_in
    a, q, k, v = jnp.split(z, [A_WIDTH, A_WIDTH + Q_WIDTH, A_WIDTH + Q_WIDTH + KV_WIDTH], axis=-1)
    q = rms_norm(q.reshape(B, L, N_HEADS, HEAD_DIM), q_g)
    k = rms_norm(k.reshape(B, L, N_KV_HEADS, HEAD_DIM), k_g)
    v = v.reshape(B, L, N_KV_HEADS, HEAD_DIM)
    return a, q, k, v


def fourier_mix(a, w, b):
    B, L, _ = a.shape
    af = a.reshape(B, L, A_GROUPS, A_GROUP_DIM).transpose(0, 2, 1, 3).astype(jnp.float32)
    f = jnp.fft.fft2(af, norm='ortho').real
    y = jnp.einsum('bglc,gcd->blgd', f, w) + b
    return y.reshape(B, L, A_WIDTH).astype(a.dtype)


def even_output(a, o, fw, fb, w_out):
    B, L, _ = a.shape
    ya = fourier_mix(a, fw, fb)
    return jnp.concatenate([ya, o.reshape(B, L, Q_WIDTH)], axis=-1) @ w_out


def chunk_gmlp(h, w_in, v_g, v_b, ws, bs, w_out):
    B, L, _ = h.shape
    z = jax.nn.gelu(h @ w_in)
    u, v = jnp.split(z, 2, axis=-1)
    v = layer_norm(v, v_g, v_b)
    vr = v.reshape(B, L // CHUNK, CHUNK, C_GROUPS, C_GROUP_DIM)
    mixed = jnp.einsum('gpq,bnqgc->bnpgc', ws, vr) + bs.T[:, :, None]
    return (u * mixed.reshape(B, L, C_WIDTH)) @ w_out


def conv_ffn(h, w_up, conv_w, conv_b, w_down):
    z = h @ w_up
    z = lax.conv_general_dilated(z, conv_w[:, None, :], (1,), 'SAME',
                                 dimension_numbers=('NWC', 'WIO', 'NWC'),
                                 feature_group_count=z.shape[-1]) + conv_b
    g, val = jnp.split(z, 2, axis=-1)
    return (jax.nn.silu(g) * val) @ w_down


def setup_inputs(seed: int = 0) -> dict:
    key = jax.random.key(seed)
    ks = jax.random.split(key, 32)
    nrm = lambda k, shape, s: jax.random.normal(k, shape, jnp.float32) * s
    D = D_MODEL
    return {
        'x_prompt': nrm(ks[0], (BATCH, SEQ, D), 1.0),
        'x_sample': nrm(ks[1], (DEC_BATCH, DEC_SEQ, D), 1.0),
        'cache_k': nrm(ks[2], (DEC_BATCH, N_EVEN, PAST_LEN, N_KV_HEADS, HEAD_DIM), 1.0),
        'cache_v': nrm(ks[3], (DEC_BATCH, N_EVEN, PAST_LEN, N_KV_HEADS, HEAD_DIM), 1.0),
        'c': nrm(ks[4], (DEC_BATCH, D), 1.0),
        'c_ctx': nrm(ks[5], (D,), 1.0),
        'w_mod': nrm(ks[6], (DEPTH, D, N_MOD * D), MOD_SCALE * D ** -0.5),
        'b_mod': nrm(ks[7], (DEPTH, N_MOD * D), 0.01),
        'ln_g': 1.0 + nrm(ks[8], (DEPTH, 2, D), 0.02),
        'ln_b': nrm(ks[9], (DEPTH, 2, D), 0.02),
        'attn_w_in': nrm(ks[10], (N_EVEN, D, EVEN_IN), D ** -0.5),
        'fourier_w': nrm(ks[11], (N_EVEN, A_GROUPS, A_GROUP_DIM, A_GROUP_DIM), A_GROUP_DIM ** -0.5),
        'fourier_b': nrm(ks[12], (N_EVEN, A_GROUPS, A_GROUP_DIM), 0.02),
        'q_norm_g': 1.0 + nrm(ks[13], (N_EVEN, HEAD_DIM), 0.02),
        'k_norm_g': 1.0 + nrm(ks[14], (N_EVEN, HEAD_DIM), 0.02),
        'attn_w_out': nrm(ks[15], (N_EVEN, A_WIDTH + Q_WIDTH, D), DEEPNORM_BETA * (A_WIDTH + Q_WIDTH) ** -0.5),
        'gmlp_w_in': nrm(ks[16], (N_ODD, D, 2 * C_WIDTH), D ** -0.5),
        'gmlp_v_ln_g': 1.0 + nrm(ks[17], (N_ODD, C_WIDTH), 0.02),
        'gmlp_v_ln_b': nrm(ks[18], (N_ODD, C_WIDTH), 0.02),
        'gmlp_ws': nrm(ks[19], (N_ODD, C_GROUPS, CHUNK, CHUNK), CHUNK ** -0.5),
        'gmlp_bs': 1.0 + nrm(ks[20], (N_ODD, C_GROUPS, CHUNK), 0.1),
        'gmlp_w_out': nrm(ks[21], (N_ODD, C_WIDTH, D), DEEPNORM_BETA * C_WIDTH ** -0.5),
        'ffn_w_up': nrm(ks[22], (DEPTH, D, 2 * FFN_DIM), D ** -0.5),
        'ffn_conv_w': nrm(ks[23], (DEPTH, CONV_WIDTH, 2 * FFN_DIM), CONV_WIDTH ** -0.5),
        'ffn_conv_b': nrm(ks[24], (DEPTH, 2 * FFN_DIM), 0.02),
        'ffn_w_down': nrm(ks[25], (DEPTH, FFN_DIM, D), DEEPNORM_BETA * FFN_DIM ** -0.5),
    }


def reference(x_prompt, x_sample, cache_k, cache_v, c, c_ctx, w_mod, b_mod, ln_g, ln_b,
              attn_w_in, fourier_w, fourier_b, q_norm_g, k_norm_g, attn_w_out,
              gmlp_w_in, gmlp_v_ln_g, gmlp_v_ln_b, gmlp_ws, gmlp_bs, gmlp_w_out,
              ffn_w_up, ffn_conv_w, ffn_conv_b, ffn_w_down):
    rope_cos, rope_sin = axial_rope_tables(x_sample.shape[1])
    xp, xs = x_prompt, x_sample
    new_k, new_v = [], []
    for l in range(DEPTH):
        i = l // 2
        sp, cp, gp, sp2, cp2, gp2 = adaln(c_ctx[None, :], w_mod[l], b_mod[l])
        ss, cs, gs, ss2, cs2, gs2 = adaln(c, w_mod[l], b_mod[l])
        hp = modulate(xp, sp, cp)
        hs = modulate(xs, ss, cs)
        if l % 2 == 0:
            a, q, k, v = even_project(hp, attn_w_in[i], q_norm_g[i], k_norm_g[i])
            yp = even_output(a, blocked_attend(q, k, v), fourier_w[i], fourier_b[i], attn_w_out[i])
            new_k.append(k)
            new_v.append(v)
            a, q, k, v = even_project(hs, attn_w_in[i], q_norm_g[i], k_norm_g[i])
            q = apply_axial_rope(q, rope_cos, rope_sin)
            k = apply_axial_rope(k, rope_cos, rope_sin)
            k_all = jnp.concatenate([cache_k[:, i], k], axis=1)
            v_all = jnp.concatenate([cache_v[:, i], v], axis=1)
            ys = even_output(a, blocked_attend(q, k_all, v_all), fourier_w[i], fourier_b[i], attn_w_out[i])
        else:
            yp = chunk_gmlp(hp, gmlp_w_in[i], gmlp_v_ln_g[i], gmlp_v_ln_b[i], gmlp_ws[i], gmlp_bs[i], gmlp_w_out[i])
            ys = chunk_gmlp(hs, gmlp_w_in[i], gmlp_v_ln_g[i], gmlp_v_ln_b[i], gmlp_ws[i], gmlp_bs[i], gmlp_w_out[i])
        xp = residual_norm(xp, yp, gp, ln_g[l, 0], ln_b[l, 0])
        xs = residual_norm(xs, ys, gs, ln_g[l, 0], ln_b[l, 0])
        fp = conv_ffn(modulate(xp, sp2, cp2), ffn_w_up[l], ffn_conv_w[l], ffn_conv_b[l], ffn_w_down[l])
        fs = conv_ffn(modulate(xs, ss2, cs2), ffn_w_up[l], ffn_conv_w[l], ffn_conv_b[l], ffn_w_down[l])
        xp = residual_norm(xp, fp, gp2, ln_g[l, 1], ln_b[l, 1])
        xs = residual_norm(xs, fs, gs2, ln_g[l, 1], ln_b[l, 1])
    state_k = jnp.stack(new_k, axis=1)
    state_v = jnp.stack(new_v, axis=1)
    return (xp, xs, state_k, state_v)
```

```python
import functools
import math
from typing import NamedTuple

import jax
import jax.numpy as jnp
from jax import lax
from jax.experimental import pallas as pl
from jax.experimental.pallas import tpu as pltpu

F32 = jnp.float32
BF16 = jnp.bfloat16

HEAD_DIM = 128
GRID_W = 64
CHUNK = 128
A_GROUPS = 4
C_GROUPS = 8
Q_PER_KV = 3
N_MOD = 6
ROPE_BASE = 10000.0
LN_EPS = 1e-6
COND_ROWS = 16
CONV_HALO = 16
EVEN_IN_COLS = 512
V7X_VMEM_LIMIT_BYTES = 56 * 1024 * 1024


class _Stream(NamedTuple):
    seq: int
    cond_base: int
    per_seq: bool

    def cond_row(self, i, bm):
        return self.cond_base + ((i * bm) // self.seq if self.per_seq else 0)

    def row_block(self, tokens, pref):
        full = self.seq if self.per_seq else tokens
        b = min(full, pref)
        assert full % b == 0 and (b % self.seq == 0 or self.seq % b == 0)
        return b


def _pick(dim, pref):
    b = min(dim, pref)
    assert dim % b == 0, (dim, pref)
    return b


def _params(*sem):
    return pltpu.CompilerParams(dimension_semantics=sem, vmem_limit_bytes=V7X_VMEM_LIMIT_BYTES)


def _standardize(x):
    xc = x - jnp.mean(x, axis=-1, keepdims=True)
    return xc * lax.rsqrt(jnp.mean(xc * xc, axis=-1, keepdims=True) + LN_EPS)


def _dot(a, b):
    return jnp.dot(a, b, preferred_element_type=F32)


def _mod_spec(stream, bm, d, j):
    return pl.BlockSpec((None, 1, d), lambda i, *_: (stream.cond_row(i, bm) * N_MOD + j, 0, 0))


def _adaln_kernel(c_ref, w_ref, b_ref, o_ref):
    c = c_ref[...]
    s = (c * jax.nn.sigmoid(c)).astype(BF16)
    o_ref[...] = _dot(s, w_ref[...].astype(BF16)) + b_ref[...]


def _adaln(cond, w_mod, b_mod):
    depth, d, n = w_mod.shape
    rows = cond.shape[0]
    bn = _pick(n, 512)
    return pl.pallas_call(
        _adaln_kernel,
        grid=(depth, n // bn),
        in_specs=[pl.BlockSpec((rows, d), lambda l, j: (0, 0)),
                  pl.BlockSpec((None, d, bn), lambda l, j: (l, 0, j)),
                  pl.BlockSpec((None, 1, bn), lambda l, j: (l, 0, j))],
        out_specs=pl.BlockSpec((None, rows, bn), lambda l, j: (l, 0, j)),
        out_shape=jax.ShapeDtypeStruct((depth, rows, n), F32),
        compiler_params=_params("parallel", "parallel"),
        name="adaln",
    )(cond, w_mod, b_mod.reshape(depth, 1, n))


def _modulate_kernel(x_ref, sh_ref, sc_ref, h_ref):
    h_ref[...] = (_standardize(x_ref[...]) * (1.0 + sc_ref[...]) + sh_ref[...]).astype(BF16)


def _modulate(x, mod, stream, j_shift):
    t, d = x.shape
    bm = stream.row_block(t, 256)
    row = pl.BlockSpec((bm, d), lambda i: (i, 0))
    return pl.pallas_call(
        _modulate_kernel,
        grid=(t // bm,),
        in_specs=[row, _mod_spec(stream, bm, d, j_shift), _mod_spec(stream, bm, d, j_shift + 1)],
        out_specs=row,
        out_shape=jax.ShapeDtypeStruct((t, d), BF16),
        compiler_params=_params("parallel"),
        name="modulate",
    )(x, mod, mod)


def _resnorm_kernel(*refs, alpha, with_mod):
    if with_mod:
        x_ref, y_ref, gate_ref, g_ref, b_ref, sh_ref, sc_ref, xo_ref, h_ref = refs
    else:
        x_ref, y_ref, gate_ref, g_ref, b_ref, xo_ref = refs
    r = alpha * x_ref[...] + gate_ref[...] * y_ref[...]
    xn = _standardize(r) * g_ref[...] + b_ref[...]
    xo_ref[...] = xn
    if with_mod:
        h_ref[...] = (_standardize(xn) * (1.0 + sc_ref[...]) + sh_ref[...]).astype(BF16)


def _resnorm(x, y, mod, stream, j_gate, ln_g, ln_b, alpha, next_mod=None, j_next=None):
    t, d = x.shape
    bm = stream.row_block(t, 256)
    row = pl.BlockSpec((bm, d), lambda i: (i, 0))
    vec = pl.BlockSpec((1, d), lambda i: (0, 0))
    with_mod = next_mod is not None
    in_specs = [row, row, _mod_spec(stream, bm, d, j_gate), vec, vec]
    args = [x, y, mod, ln_g.reshape(1, d), ln_b.reshape(1, d)]
    out_specs = [row]
    out_shape = [jax.ShapeDtypeStruct((t, d), F32)]
    if with_mod:
        in_specs += [_mod_spec(stream, bm, d, j_next), _mod_spec(stream, bm, d, j_next + 1)]
        args += [next_mod, next_mod]
        out_specs.append(row)
        out_shape.append(jax.ShapeDtypeStruct((t, d), BF16))
    out = pl.pallas_call(
        functools.partial(_resnorm_kernel, alpha=alpha, with_mod=with_mod),
        grid=(t // bm,),
        in_specs=in_specs,
        out_specs=out_specs,
        out_shape=out_shape,
        compiler_params=_params("parallel"),
        name="resnorm",
    )(*args)
    return (out[0], out[1]) if with_mod else (out[0], None)


def _gelu_tanh(x):
    return x * (0.5 * (1.0 + jnp.tanh(math.sqrt(2.0 / math.pi) * (x + 0.044715 * (x * x * x)))))


def _mm_kernel(*refs, n_pairs, gelu):
    o_ref = refs[2 * n_pairs]
    acc = _dot(refs[0][...], refs[n_pairs][...])
    for p in range(1, n_pairs):
        acc = acc + _dot(refs[p][...], refs[n_pairs + p][...])
    if gelu:
        acc = _gelu_tanh(acc)
    o_ref[...] = acc.astype(o_ref.dtype)


def _mm(pairs, bm, bn, out_dtype=F32, gelu=False, name="mm"):
    m = pairs[0][0].shape[0]
    n = pairs[0][1].shape[1]
    bm, bn = _pick(m, bm), _pick(n, bn)
    x_specs = [pl.BlockSpec((bm, x.shape[1]), lambda i, j: (i, 0)) for x, _ in pairs]
    w_specs = [pl.BlockSpec((w.shape[0], bn), lambda i, j: (0, j)) for _, w in pairs]
    return pl.pallas_call(
        functools.partial(_mm_kernel, n_pairs=len(pairs), gelu=gelu),
        grid=(m // bm, n // bn),
        in_specs=x_specs + w_specs,
        out_specs=pl.BlockSpec((bm, bn), lambda i, j: (i, j)),
        out_shape=jax.ShapeDtypeStruct((m, n), out_dtype),
        compiler_params=_params("parallel", "parallel"),
        name=name,
    )(*[x for x, _ in pairs], *[w for _, w in pairs])


def _head_rms(z, gain):
    return z * lax.rsqrt(jnp.mean(z * z, axis=-1, keepdims=True) + LN_EPS) * gain


def _rope(z, c_ref, sa_ref, sb_ref):
    hd = z.shape[-1]
    return (z * c_ref[...] + pltpu.roll(z, hd - hd // 4, axis=1) * sa_ref[...]
            + pltpu.roll(z, hd // 4, axis=1) * sb_ref[...])


def _even_in_kernel(*refs, heads_per_block, nb, rope, state, q_scale):
    k_lo, v_lo = (1 + Q_PER_KV) * nb, (2 + Q_PER_KV) * nb
    h_ref, w_ref, qg_ref, kg_ref = refs[:4]
    refs = refs[4:]
    if rope:
        c_ref, sa_ref, sb_ref = refs[:3]
        refs = refs[3:]
    z_ref = refs[0]
    if state:
        k32_ref, v32_ref = refs[1:3]
    j = pl.program_id(1)
    z = _dot(h_ref[...], w_ref[...])

    def heads(gain_ref, scale, k32=None):
        for hh in range(heads_per_block):
            sl = slice(hh * HEAD_DIM, (hh + 1) * HEAD_DIM)
            n = _head_rms(z[:, sl], gain_ref[...])
            if k32 is not None:
                k32[:, sl] = n
            if rope:
                n = _rope(n, c_ref, sa_ref, sb_ref)
            if scale != 1.0:
                n = n * scale
            z_ref[:, sl] = n.astype(BF16)

    @pl.when(jnp.logical_or(j < nb, j >= v_lo))
    def _():
        z_ref[...] = z.astype(BF16)

    @pl.when(jnp.logical_and(j >= nb, j < k_lo))
    def _():
        heads(qg_ref, q_scale)

    @pl.when(jnp.logical_and(j >= k_lo, j < v_lo))
    def _():
        heads(kg_ref, 1.0, k32_ref if state else None)

    if state:
        @pl.when(j >= v_lo)
        def _():
            v32_ref[...] = z


def _even_in(h, w_in, q_g, k_g, stream, rope_tabs=None, state=False):
    t, d = h.shape
    n = w_in.shape[1]
    quarter = d // 4
    assert n == (3 + Q_PER_KV) * quarter
    bn = _pick(quarter, EVEN_IN_COLS)
    nb = quarter // bn
    k_lo, v_lo = (1 + Q_PER_KV) * nb, (2 + Q_PER_KV) * nb
    bm = stream.row_block(t, 1024)
    vec = pl.BlockSpec((1, HEAD_DIM), lambda i, j: (0, 0))
    in_specs = [pl.BlockSpec((bm, d), lambda i, j: (i, 0)),
                pl.BlockSpec((d, bn), lambda i, j: (0, j)), vec, vec]
    args = [h, w_in, q_g.reshape(1, HEAD_DIM), k_g.reshape(1, HEAD_DIM)]
    if rope_tabs is not None:
        blocks_per_seq = stream.seq // bm
        tab = pl.BlockSpec((bm, HEAD_DIM), lambda i, j: (i % blocks_per_seq, 0))
        in_specs += [tab, tab, tab]
        args += list(rope_tabs)
    out_specs = [pl.BlockSpec((bm, bn), lambda i, j: (i, j))]
    out_shape = [jax.ShapeDtypeStruct((t, n), BF16)]
    if state:
        out_specs += [pl.BlockSpec((bm, bn), lambda i, j: (i, jnp.clip(j - k_lo, 0, nb - 1))),
                      pl.BlockSpec((bm, bn), lambda i, j: (i, jnp.clip(j - v_lo, 0, nb - 1)))]
        out_shape += [jax.ShapeDtypeStruct((t, quarter), F32)] * 2
    return pl.pallas_call(
        functools.partial(_even_in_kernel, heads_per_block=bn // HEAD_DIM, nb=nb, rope=rope_tabs is not None,
                          state=state, q_scale=HEAD_DIM ** -0.5),
        grid=(t // bm, n // bn),
        in_specs=in_specs,
        out_specs=out_specs,
        out_shape=out_shape,
        compiler_params=_params("parallel", "arbitrary"),
        name="even_in",
    )(*args)


def _rope_tables(seq):
    rows = (jnp.arange(seq) // GRID_W).astype(F32)
    cols = (jnp.arange(seq) % GRID_W).astype(F32)
    nf = HEAD_DIM // 4
    inv = ROPE_BASE ** (-jnp.arange(nf, dtype=F32) / nf)
    lane = jnp.arange(HEAD_DIM)
    pos = jnp.where((lane // (2 * nf))[None, :] == 0, rows[:, None], cols[:, None])
    ang = pos * inv[lane % nf][None, :]
    first = ((lane // nf) % 2 == 0)[None, :]
    c, s = jnp.cos(ang), jnp.sin(ang)
    return c, jnp.where(first, -s, 0.0), jnp.where(first, 0.0, s)


def _attn_kernel(*refs, cached, bq):
    q_refs = refs[:Q_PER_KV]
    k_ref, v_ref = refs[Q_PER_KV:Q_PER_KV + 2]
    if cached:
        ck_ref, cv_ref = refs[Q_PER_KV + 2:Q_PER_KV + 4]
    o_ref = refs[-1]
    dims = (((1,), (1,)), ((), ()))
    q = jnp.concatenate([r[...] for r in q_refs], axis=0)
    s = lax.dot_general(q, k_ref[...], dims, preferred_element_type=F32)
    m = jnp.max(s, axis=-1, keepdims=True)
    if cached:
        sc = lax.dot_general(q, ck_ref[...].astype(BF16), dims, preferred_element_type=F32)
        m = jnp.maximum(m, jnp.max(sc, axis=-1, keepdims=True))
    p = jnp.exp(s - m)
    l = jnp.sum(p, axis=-1, keepdims=True)
    acc = _dot(p.astype(BF16), v_ref[...])
    if cached:
        pc = jnp.exp(sc - m)
        l = l + jnp.sum(pc, axis=-1, keepdims=True)
        acc = acc + _dot(pc.astype(BF16), cv_ref[...].astype(BF16))
    o = acc / l
    for t in range(Q_PER_KV):
        o_ref[:, t * HEAD_DIM:(t + 1) * HEAD_DIM] = o[t * bq:(t + 1) * bq].astype(BF16)


def _attention(z, batch, seq, d, cache_k=None, cache_v=None):
    a_blocks = (d // 4) // HEAD_DIM
    n_kv = (d // 4) // HEAD_DIM
    n_q = Q_PER_KV * n_kv
    bq = _pick(seq, 256)
    nqb = seq // bq
    cached = cache_k is not None

    def q_spec(t):
        return pl.BlockSpec((bq, HEAD_DIM), lambda b, g, r: (b * nqb + r, a_blocks + Q_PER_KV * g + t))

    in_specs = [q_spec(t) for t in range(Q_PER_KV)]
    in_specs += [pl.BlockSpec((seq, HEAD_DIM), lambda b, g, r: (b, a_blocks + n_q + g)),
                 pl.BlockSpec((seq, HEAD_DIM), lambda b, g, r: (b, a_blocks + n_q + n_kv + g))]
    args = [z] * (Q_PER_KV + 2)
    if cached:
        past = cache_k.shape[1]
        cspec = pl.BlockSpec((None, past, HEAD_DIM), lambda b, g, r: (b, 0, g))
        in_specs += [cspec, cspec]
        args += [cache_k, cache_v]
    return pl.pallas_call(
        functools.partial(_attn_kernel, cached=cached, bq=bq),
        grid=(batch, n_kv, nqb),
        in_specs=in_specs,
        out_specs=pl.BlockSpec((bq, Q_PER_KV * HEAD_DIM), lambda b, g, r: (b * nqb + r, g)),
        out_shape=jax.ShapeDtypeStruct((batch * seq, n_q * HEAD_DIM), BF16),
        compiler_params=_params("parallel", "parallel", "arbitrary"),
        name="attention",
    )(*args)


def _dft_mats(n):
    k = jnp.arange(n, dtype=jnp.int32)
    ang = ((k[:, None] * k[None, :]) % n).astype(F32) * (2.0 * math.pi / n)
    return jnp.cos(ang).astype(BF16), jnp.sin(ang).astype(BF16)


def _chan_dft_kernel(a_ref, cd_ref, sd_ref, pq_ref, *, groups):
    aw = a_ref.shape[1]
    dg = aw // groups
    for g in range(groups):
        ag = a_ref[:, g * dg:(g + 1) * dg]
        pq_ref[:, g * dg:(g + 1) * dg] = _dot(ag, cd_ref[...]).astype(BF16)
        pq_ref[:, aw + g * dg:aw + (g + 1) * dg] = _dot(ag, sd_ref[...]).astype(BF16)


def _chan_dft(z, aw):
    t = z.shape[0]
    dg = aw // A_GROUPS
    bm = _pick(t, 1024)
    cd, sd = _dft_mats(dg)
    full = pl.BlockSpec((dg, dg), lambda i: (0, 0))
    return pl.pallas_call(
        functools.partial(_chan_dft_kernel, groups=A_GROUPS),
        grid=(t // bm,),
        in_specs=[pl.BlockSpec((bm, aw), lambda i: (i, 0)), full, full],
        out_specs=pl.BlockSpec((bm, 2 * aw), lambda i: (i, 0)),
        out_shape=jax.ShapeDtypeStruct((t, 2 * aw), BF16),
        compiler_params=_params("parallel"),
        name="chan_dft",
    )(z, cd, sd)


def _seq_dft_kernel(pq_ref, cl_ref, sl_ref, fw_ref, fb_ref, y_ref, *, groups, norm):
    aw = y_ref.shape[1]
    dg = aw // groups
    f = (_dot(cl_ref[...], pq_ref[:, :aw]) - _dot(sl_ref[...], pq_ref[:, aw:])) * norm
    fb = fb_ref[...]
    for g in range(groups):
        sl = slice(g * dg, (g + 1) * dg)
        y_ref[:, sl] = (_dot(f[:, sl].astype(BF16), fw_ref[g]) + fb[:, sl]).astype(BF16)


def _seq_dft(pq, batch, seq, fw, fb):
    aw = pq.shape[1] // 2
    dg = aw // A_GROUPS
    bl = _pick(seq, 512)
    nlb = seq // bl
    cl, sl = _dft_mats(seq)
    mat = pl.BlockSpec((bl, seq), lambda b, r: (r, 0))
    return pl.pallas_call(
        functools.partial(_seq_dft_kernel, groups=A_GROUPS, norm=1.0 / math.sqrt(seq * dg)),
        grid=(batch, nlb),
        in_specs=[pl.BlockSpec((seq, 2 * aw), lambda b, r: (b, 0), pipeline_mode=pl.Buffered(1)),
                  mat, mat,
                  pl.BlockSpec((A_GROUPS, dg, dg), lambda b, r: (0, 0, 0)),
                  pl.BlockSpec((1, aw), lambda b, r: (0, 0))],
        out_specs=pl.BlockSpec((bl, aw), lambda b, r: (b * nlb + r, 0)),
        out_shape=jax.ShapeDtypeStruct((batch * seq, aw), BF16),
        compiler_params=_params("parallel", "arbitrary"),
        name="seq_dft",
    )(pq, cl, sl, fw, fb.reshape(1, aw))


def _ffn_up_kernel(hp_ref, h_ref, hn_ref, wg_ref, wv_ref, cwg_ref, cwv_ref, cbg_ref, cbv_ref,
                   u_ref, hh_ref, zg_ref, zv_ref, *, bm, seq):
    i = pl.program_id(0)
    halo = CONV_HALO

    @pl.when(pl.program_id(1) == 0)
    def _():
        hh_ref[0:halo] = hp_ref[...]
        hh_ref[halo:halo + bm] = h_ref[...]
        hh_ref[halo + bm:halo + bm + halo] = hn_ref[...]

    hh = hh_ref[...]
    zg_ref[...] = _dot(hh, wg_ref[...])
    zv_ref[...] = _dot(hh, wv_ref[...])
    pos = (i * bm + lax.broadcasted_iota(jnp.int32, (bm, 1), 0)) % seq
    not_first = (pos != 0).astype(F32)
    not_last = (pos != seq - 1).astype(F32)

    def conv(z_ref, cw_ref, cb_ref):
        return ((z_ref[halo - 1:halo - 1 + bm] * not_first) * cw_ref[0:1]
                + z_ref[halo:halo + bm] * cw_ref[1:2]
                + (z_ref[halo + 1:halo + 1 + bm] * not_last) * cw_ref[2:3] + cb_ref[...])

    g = conv(zg_ref, cwg_ref, cbg_ref)
    v = conv(zv_ref, cwv_ref, cbv_ref)
    u_ref[...] = (g * jax.nn.sigmoid(g) * v).astype(BF16)


def _ffn_up(h, w_up, conv_w, conv_b, stream):
    t, d = h.shape
    f = w_up.shape[1] // 2
    bm = stream.row_block(t, 1024)
    bn = _pick(f, 256)
    nj = f // bn
    halo = CONV_HALO
    hb = bm // halo
    last = t // halo - 1
    assert bm % halo == 0 and conv_w.shape[0] == 3
    return pl.pallas_call(
        functools.partial(_ffn_up_kernel, bm=bm, seq=stream.seq),
        grid=(t // bm, nj),
        in_specs=[pl.BlockSpec((halo, d), lambda i, j: (jnp.maximum(i * hb - 1, 0), 0)),
                  pl.BlockSpec((bm, d), lambda i, j: (i, 0)),
                  pl.BlockSpec((halo, d), lambda i, j: (jnp.minimum((i + 1) * hb, last), 0)),
                  pl.BlockSpec((d, bn), lambda i, j: (0, j)),
                  pl.BlockSpec((d, bn), lambda i, j: (0, j + nj)),
                  pl.BlockSpec((3, bn), lambda i, j: (0, j)),
                  pl.BlockSpec((3, bn), lambda i, j: (0, j + nj)),
                  pl.BlockSpec((1, bn), lambda i, j: (0, j)),
                  pl.BlockSpec((1, bn), lambda i, j: (0, j + nj))],
        out_specs=pl.BlockSpec((bm, bn), lambda i, j: (i, j)),
        out_shape=jax.ShapeDtypeStruct((t, f), BF16),
        scratch_shapes=[pltpu.VMEM((bm + 2 * halo, d), BF16),
                        pltpu.VMEM((bm + 2 * halo, bn), F32),
                        pltpu.VMEM((bm + 2 * halo, bn), F32)],
        compiler_params=_params("parallel", "arbitrary"),
        name="ffn_up",
    )(h, h, h, w_up, w_up, conv_w, conv_w, conv_b.reshape(1, 2 * f), conv_b.reshape(1, 2 * f))


def _gmlp_mix_kernel(u_ref, v_ref, g_ref, b_ref, ws_ref, bst_ref, o_ref, *, groups):
    bm, c = v_ref.shape
    cg = c // groups
    vn = (_standardize(v_ref[...]) * g_ref[...] + b_ref[...]).astype(BF16)
    for ch in range(bm // CHUNK):
        rows = slice(ch * CHUNK, (ch + 1) * CHUNK)
        for g in range(groups):
            cols = slice(g * cg, (g + 1) * cg)
            mixed = _dot(ws_ref[g], vn[rows, cols]) + bst_ref[:, g:g + 1]
            o_ref[rows, cols] = (u_ref[rows, cols] * mixed).astype(BF16)


def _gmlp_mix(uv, v_g, v_b, ws, bs, stream):
    t, c2 = uv.shape
    c = c2 // 2
    bm = _pick(stream.seq, 256)
    assert bm % CHUNK == 0
    vec = pl.BlockSpec((1, c), lambda i: (0, 0))
    return pl.pallas_call(
        functools.partial(_gmlp_mix_kernel, groups=C_GROUPS),
        grid=(t // bm,),
        in_specs=[pl.BlockSpec((bm, c), lambda i: (i, 0)),
                  pl.BlockSpec((bm, c), lambda i: (i, 1)),
                  vec, vec,
                  pl.BlockSpec((C_GROUPS, CHUNK, CHUNK), lambda i: (0, 0, 0)),
                  pl.BlockSpec((CHUNK, C_GROUPS), lambda i: (0, 0))],
        out_specs=pl.BlockSpec((bm, c), lambda i: (i, 0)),
        out_shape=jax.ShapeDtypeStruct((t, c), BF16),
        compiler_params=_params("parallel"),
        name="gmlp_mix",
    )(uv, uv, v_g.reshape(1, c), v_b.reshape(1, c), ws, bs.T)


def kernel(x_prompt, x_sample, cache_k, cache_v, c, c_ctx, w_mod, b_mod, ln_g, ln_b, attn_w_in, fourier_w, fourier_b, q_norm_g, k_norm_g, attn_w_out, gmlp_w_in, gmlp_v_ln_g, gmlp_v_ln_b, gmlp_ws, gmlp_bs, gmlp_w_out, ffn_w_up, ffn_conv_w, ffn_conv_b, ffn_w_down):
    batch, seq, d = x_prompt.shape
    dec_batch, dec_seq, _ = x_sample.shape
    depth = w_mod.shape[0]
    n_kv = (d // 4) // HEAD_DIM
    aw = d // 4
    alpha = (2 * depth) ** 0.25
    assert dec_batch + 1 <= COND_ROWS and dec_seq % GRID_W == 0

    streams = (_Stream(seq, 0, False), _Stream(dec_seq, 1, True))
    xs = [x_prompt.reshape(batch * seq, d), x_sample.reshape(dec_batch * dec_seq, d)]
    batches = (batch, dec_batch)

    cond = jnp.zeros((COND_ROWS, d), F32).at[0].set(c_ctx).at[1:1 + dec_batch].set(c)
    mods = _adaln(cond, w_mod, b_mod).reshape(depth, COND_ROWS * N_MOD, 1, d)
    rope_tabs = _rope_tables(dec_seq)
    past = cache_k.shape[2]
    new_k, new_v = [], []

    hs = [_modulate(xs[s], mods[0], streams[s], 0) for s in range(2)]
    for l in range(depth):
        i = l // 2
        mod = mods[l]
        ys = []
        if l % 2 == 0:
            w_in = attn_w_in[i].astype(BF16)
            w_out_a = attn_w_out[i, :aw].astype(BF16)
            w_out_o = attn_w_out[i, aw:].astype(BF16)
            fw = fourier_w[i].astype(BF16)
            fb = fourier_b[i].reshape(aw)
            ck = cache_k[:, i].reshape(dec_batch, past, n_kv * HEAD_DIM)
            cv = cache_v[:, i].reshape(dec_batch, past, n_kv * HEAD_DIM)
            for s in range(2):
                st = streams[s]
                if s == 0:
                    z, k32, v32 = _even_in(hs[s], w_in, q_norm_g[i], k_norm_g[i], st, state=True)
                    new_k.append(k32.reshape(batch, seq, n_kv, HEAD_DIM))
                    new_v.append(v32.reshape(batch, seq, n_kv, HEAD_DIM))
                    o = _attention(z, batch, seq, d)
                else:
                    (z,) = _even_in(hs[s], w_in, q_norm_g[i], k_norm_g[i], st, rope_tabs=rope_tabs)
                    o = _attention(z, dec_batch, dec_seq, d, ck, cv)
                ya = _seq_dft(_chan_dft(z, aw), batches[s], st.seq, fw, fb)
                ys.append(_mm([(ya, w_out_a), (o, w_out_o)], 1024, 1024, name="even_out"))
        else:
            w_in = gmlp_w_in[i].astype(BF16)
            w_out = gmlp_w_out[i].astype(BF16)
            ws = gmlp_ws[i].astype(BF16)
            for s in range(2):
                uv = _mm([(hs[s], w_in)], 1024, 1024, gelu=True, name="gmlp_in")
                gated = _gmlp_mix(uv, gmlp_v_ln_g[i], gmlp_v_ln_b[i], ws, gmlp_bs[i], streams[s])
                ys.append(_mm([(gated, w_out)], 1024, 1024, name="gmlp_out"))

        w_up = ffn_w_up[l].astype(BF16)
        w_down = ffn_w_down[l].astype(BF16)
        for s in range(2):
            st = streams[s]
            xs[s], h2 = _resnorm(xs[s], ys[s], mod, st, 2, ln_g[l, 0], ln_b[l, 0], alpha, mod, 3)
            u = _ffn_up(h2, w_up, ffn_conv_w[l], ffn_conv_b[l], st)
            f = _mm([(u, w_down)], 512, 512, name="ffn_down")
            nxt = mods[l + 1] if l + 1 < depth else None
            xs[s], hs[s] = _resnorm(xs[s], f, mod, st, 5, ln_g[l, 1], ln_b[l, 1], alpha, nxt, 0)

    state_k = jnp.stack(new_k, axis=1)
    state_v = jnp.stack(new_v, axis=1)
    return (xs[0].reshape(batch, seq, d), xs[1].reshape(dec_batch, dec_seq, d), state_k, state_v)
```

```python
import functools
import math
from typing import NamedTuple

import jax
import jax.numpy as jnp
from jax import lax
from jax.experimental import pallas as pl
from jax.experimental.pallas import tpu as pltpu

F32 = jnp.float32
BF16 = jnp.bfloat16

HEAD_DIM = 128
GRID_W = 64
CHUNK = 128
A_GROUPS = 4
C_GROUPS = 8
Q_PER_KV = 3
N_MOD = 6
ROPE_BASE = 10000.0
LN_EPS = 1e-6
COND_ROWS = 16
CONV_HALO = 16
EVEN_IN_COLS = 512
FFN_COLS = 512
V7X_VMEM_LIMIT_BYTES = 56 * 1024 * 1024

class _Stream(NamedTuple):
    seq: int
    cond_base: int
    per_seq: bool

    def cond_row(self, i, bm):
        return self.cond_base + ((i * bm) // self.seq if self.per_seq else 0)

    def row_block(self, tokens, pref):
        full = self.seq if self.per_seq else tokens
        b = min(full, pref)
        assert full % b == 0 and (b % self.seq == 0 or self.seq % b == 0)
        return b


def _pick(dim, pref):
    b = min(dim, pref)
    assert dim % b == 0, (dim, pref)
    return b


def _params(*sem):
    return pltpu.CompilerParams(dimension_semantics=sem, vmem_limit_bytes=V7X_VMEM_LIMIT_BYTES)


def _standardize(x):
    xc = x - jnp.mean(x, axis=-1, keepdims=True)
    return xc * lax.rsqrt(jnp.mean(xc * xc, axis=-1, keepdims=True) + LN_EPS)


def _dot(a, b):
    return jnp.dot(a, b, preferred_element_type=F32)


def _mod_spec(stream, bm, d, j):
    return pl.BlockSpec((None, 1, d), lambda i, *_: (stream.cond_row(i, bm) * N_MOD + j, 0, 0))


def _adaln_kernel(c_ref, w_ref, b_ref, o_ref):
    c = c_ref[...]
    s = (c * jax.nn.sigmoid(c)).astype(BF16)
    o_ref[...] = _dot(s, w_ref[...].astype(BF16)) + b_ref[...]


def _adaln(cond, w_mod, b_mod):
    depth, d, n = w_mod.shape
    rows = cond.shape[0]
    bn = _pick(n, 512)
    return pl.pallas_call(
        _adaln_kernel,
        grid=(depth, n // bn),
        in_specs=[pl.BlockSpec((rows, d), lambda l, j: (0, 0)),
                  pl.BlockSpec((None, d, bn), lambda l, j: (l, 0, j)),
                  pl.BlockSpec((None, 1, bn), lambda l, j: (l, 0, j))],
        out_specs=pl.BlockSpec((None, rows, bn), lambda l, j: (l, 0, j)),
        out_shape=jax.ShapeDtypeStruct((depth, rows, n), F32),
        compiler_params=_params("parallel", "parallel"),
        name="adaln",
    )(cond, w_mod, b_mod.reshape(depth, 1, n))


def _modulate_kernel(x_ref, sh_ref, sc_ref, h_ref):
    h_ref[...] = (_standardize(x_ref[...]) * (1.0 + sc_ref[...]) + sh_ref[...]).astype(BF16)


def _modulate(x, mod, stream, j_shift):
    t, d = x.shape
    bm = stream.row_block(t, 256)
    row = pl.BlockSpec((bm, d), lambda i: (i, 0))
    return pl.pallas_call(
        _modulate_kernel,
        grid=(t // bm,),
        in_specs=[row, _mod_spec(stream, bm, d, j_shift), _mod_spec(stream, bm, d, j_shift + 1)],
        out_specs=row,
        out_shape=jax.ShapeDtypeStruct((t, d), BF16),
        compiler_params=_params("parallel"),
        name="modulate",
    )(x, mod, mod)


def _resnorm_kernel(*refs, alpha, with_mod):
    if with_mod:
        x_ref, y_ref, gate_ref, g_ref, b_ref, sh_ref, sc_ref, xo_ref, h_ref = refs
    else:
        x_ref, y_ref, gate_ref, g_ref, b_ref, xo_ref = refs
    r = alpha * x_ref[...] + gate_ref[...] * y_ref[...]
    xn = _standardize(r) * g_ref[...] + b_ref[...]
    xo_ref[...] = xn
    if with_mod:
        h_ref[...] = (_standardize(xn) * (1.0 + sc_ref[...]) + sh_ref[...]).astype(BF16)


def _resnorm(x, y, mod, stream, j_gate, ln_g, ln_b, alpha, next_mod=None, j_next=None):
    t, d = x.shape
    bm = stream.row_block(t, 256)
    row = pl.BlockSpec((bm, d), lambda i: (i, 0))
    vec = pl.BlockSpec((1, d), lambda i: (0, 0))
    with_mod = next_mod is not None
    in_specs = [row, row, _mod_spec(stream, bm, d, j_gate), vec, vec]
    args = [x, y, mod, ln_g.reshape(1, d), ln_b.reshape(1, d)]
    out_specs = [row]
    out_shape = [jax.ShapeDtypeStruct((t, d), F32)]
    if with_mod:
        in_specs += [_mod_spec(stream, bm, d, j_next), _mod_spec(stream, bm, d, j_next + 1)]
        args += [next_mod, next_mod]
        out_specs.append(row)
        out_shape.append(jax.ShapeDtypeStruct((t, d), BF16))
    out = pl.pallas_call(
        functools.partial(_resnorm_kernel, alpha=alpha, with_mod=with_mod),
        grid=(t // bm,),
        in_specs=in_specs,
        out_specs=out_specs,
        out_shape=out_shape,
        compiler_params=_params("parallel"),
        name="resnorm",
    )(*args)
    return (out[0], out[1]) if with_mod else (out[0], None)


def _gelu_tanh(x):
    return x * (0.5 * (1.0 + jnp.tanh(math.sqrt(2.0 / math.pi) * (x + 0.044715 * (x * x * x)))))


def _mm_kernel(*refs, n_pairs, gelu):
    o_ref = refs[2 * n_pairs]
    acc = _dot(refs[0][...], refs[n_pairs][...])
    for p in range(1, n_pairs):
        acc = acc + _dot(refs[p][...], refs[n_pairs + p][...])
    if gelu:
        acc = _gelu_tanh(acc)
    o_ref[...] = acc.astype(o_ref.dtype)


def _mm(pairs, bm, bn, out_dtype=F32, gelu=False, name="mm"):
    m = pairs[0][0].shape[0]
    n = pairs[0][1].shape[1]
    bm, bn = _pick(m, bm), _pick(n, bn)
    x_specs = [pl.BlockSpec((bm, x.shape[1]), lambda i, j: (i, 0)) for x, _ in pairs]
    w_specs = [pl.BlockSpec((w.shape[0], bn), lambda i, j: (0, j)) for _, w in pairs]
    return pl.pallas_call(
        functools.partial(_mm_kernel, n_pairs=len(pairs), gelu=gelu),
        grid=(m // bm, n // bn),
        in_specs=x_specs + w_specs,
        out_specs=pl.BlockSpec((bm, bn), lambda i, j: (i, j)),
        out_shape=jax.ShapeDtypeStruct((m, n), out_dtype),
        compiler_params=_params("parallel", "parallel"),
        name=name,
    )(*[x for x, _ in pairs], *[w for _, w in pairs])


def _head_rms(z, gain):
    return z * lax.rsqrt(jnp.mean(z * z, axis=-1, keepdims=True) + LN_EPS) * gain


def _rope(z, c_ref, sa_ref, sb_ref):
    hd = z.shape[-1]
    return (z * c_ref[...] + pltpu.roll(z, hd - hd // 4, axis=1) * sa_ref[...]
            + pltpu.roll(z, hd // 4, axis=1) * sb_ref[...])


def _even_in_kernel(*refs, heads_per_block, nb, rope, state, q_scale):
    k_lo, v_lo = (1 + Q_PER_KV) * nb, (2 + Q_PER_KV) * nb
    h_ref, w_ref, qg_ref, kg_ref = refs[:4]
    refs = refs[4:]
    if rope:
        c_ref, sa_ref, sb_ref = refs[:3]
        refs = refs[3:]
    z_ref = refs[0]
    if state:
        k32_ref, v32_ref = refs[1:3]
    j = pl.program_id(1)
    z = _dot(h_ref[...], w_ref[...])

    def heads(gain_ref, scale, k32=None):
        for hh in range(heads_per_block):
            sl = slice(hh * HEAD_DIM, (hh + 1) * HEAD_DIM)
            n = _head_rms(z[:, sl], gain_ref[...])
            if k32 is not None:
                k32[:, sl] = n
            if rope:
                n = _rope(n, c_ref, sa_ref, sb_ref)
            if scale != 1.0:
                n = n * scale
            z_ref[:, sl] = n.astype(BF16)

    @pl.when(jnp.logical_or(j < nb, j >= v_lo))
    def _():
        z_ref[...] = z.astype(BF16)

    @pl.when(jnp.logical_and(j >= nb, j < k_lo))
    def _():
        heads(qg_ref, q_scale)

    @pl.when(jnp.logical_and(j >= k_lo, j < v_lo))
    def _():
        heads(kg_ref, 1.0, k32_ref if state else None)

    if state:
        @pl.when(j >= v_lo)
        def _():
            v32_ref[...] = z


def _even_in(h, w_in, q_g, k_g, stream, rope_tabs=None, state=False):
    t, d = h.shape
    n = w_in.shape[1]
    quarter = d // 4
    assert n == (3 + Q_PER_KV) * quarter
    bn = _pick(quarter, EVEN_IN_COLS)
    nb = quarter // bn
    k_lo, v_lo = (1 + Q_PER_KV) * nb, (2 + Q_PER_KV) * nb
    bm = stream.row_block(t, 1024)
    vec = pl.BlockSpec((1, HEAD_DIM), lambda i, j: (0, 0))
    in_specs = [pl.BlockSpec((bm, d), lambda i, j: (i, 0)),
                pl.BlockSpec((d, bn), lambda i, j: (0, j)), vec, vec]
    args = [h, w_in, q_g.reshape(1, HEAD_DIM), k_g.reshape(1, HEAD_DIM)]
    if rope_tabs is not None:
        blocks_per_seq = stream.seq // bm
        tab = pl.BlockSpec((bm, HEAD_DIM), lambda i, j: (i % blocks_per_seq, 0))
        in_specs += [tab, tab, tab]
        args += list(rope_tabs)
    out_specs = [pl.BlockSpec((bm, bn), lambda i, j: (i, j))]
    out_shape = [jax.ShapeDtypeStruct((t, n), BF16)]
    if state:
        out_specs += [pl.BlockSpec((bm, bn), lambda i, j: (i, jnp.clip(j - k_lo, 0, nb - 1))),
                      pl.BlockSpec((bm, bn), lambda i, j: (i, jnp.clip(j - v_lo, 0, nb - 1)))]
        out_shape += [jax.ShapeDtypeStruct((t, quarter), F32)] * 2
    return pl.pallas_call(
        functools.partial(_even_in_kernel, heads_per_block=bn // HEAD_DIM, nb=nb, rope=rope_tabs is not None,
                          state=state, q_scale=HEAD_DIM ** -0.5),
        grid=(t // bm, n // bn),
        in_specs=in_specs,
        out_specs=out_specs,
        out_shape=out_shape,
        compiler_params=_params("parallel", "arbitrary"),
        name="even_in",
    )(*args)


def _rope_tables(seq):
    rows = (jnp.arange(seq) // GRID_W).astype(F32)
    cols = (jnp.arange(seq) % GRID_W).astype(F32)
    nf = HEAD_DIM // 4
    inv = ROPE_BASE ** (-jnp.arange(nf, dtype=F32) / nf)
    lane = jnp.arange(HEAD_DIM)
    pos = jnp.where((lane // (2 * nf))[None, :] == 0, rows[:, None], cols[:, None])
    ang = pos * inv[lane % nf][None, :]
    first = ((lane // nf) % 2 == 0)[None, :]
    c, s = jnp.cos(ang), jnp.sin(ang)
    return c, jnp.where(first, -s, 0.0), jnp.where(first, 0.0, s)


def _attn_kernel(*refs, cached, bq):
    q_refs = refs[:Q_PER_KV]
    k_ref, v_ref = refs[Q_PER_KV:Q_PER_KV + 2]
    if cached:
        ck_ref, cv_ref = refs[Q_PER_KV + 2:Q_PER_KV + 4]
    o_ref = refs[-1]
    dims = (((1,), (1,)), ((), ()))
    k, v = k_ref[...], v_ref[...]
    if cached:
        kc, vc = ck_ref[...].astype(BF16), cv_ref[...].astype(BF16)
    for t in range(Q_PER_KV):
        q = q_refs[t][...]
        s = lax.dot_general(q, k, dims, preferred_element_type=F32)
        m = jnp.max(s, axis=-1, keepdims=True)
        if cached:
            sc = lax.dot_general(q, kc, dims, preferred_element_type=F32)
            m = jnp.maximum(m, jnp.max(sc, axis=-1, keepdims=True))
        p = jnp.exp(s - m)
        l = jnp.sum(p, axis=-1, keepdims=True)
        acc = _dot(p.astype(BF16), v)
        if cached:
            pc = jnp.exp(sc - m)
            l = l + jnp.sum(pc, axis=-1, keepdims=True)
            acc = acc + _dot(pc.astype(BF16), vc)
        o_ref[:, t * HEAD_DIM:(t + 1) * HEAD_DIM] = (acc / l).astype(BF16)


def _attention(z, batch, seq, d, cache_k=None, cache_v=None):
    a_blocks = (d // 4) // HEAD_DIM
    n_kv = (d // 4) // HEAD_DIM
    n_q = Q_PER_KV * n_kv
    bq = _pick(seq, 256)
    nqb = seq // bq
    cached = cache_k is not None

    def q_spec(t):
        return pl.BlockSpec((bq, HEAD_DIM), lambda b, g, r: (b * nqb + r, a_blocks + Q_PER_KV * g + t))

    in_specs = [q_spec(t) for t in range(Q_PER_KV)]
    in_specs += [pl.BlockSpec((seq, HEAD_DIM), lambda b, g, r: (b, a_blocks + n_q + g)),
                 pl.BlockSpec((seq, HEAD_DIM), lambda b, g, r: (b, a_blocks + n_q + n_kv + g))]
    args = [z] * (Q_PER_KV + 2)
    if cached:
        past = cache_k.shape[1]
        cspec = pl.BlockSpec((None, past, HEAD_DIM), lambda b, g, r: (b, 0, g))
        in_specs += [cspec, cspec]
        args += [cache_k, cache_v]
    return pl.pallas_call(
        functools.partial(_attn_kernel, cached=cached, bq=bq),
        grid=(batch, n_kv, nqb),
        in_specs=in_specs,
        out_specs=pl.BlockSpec((bq, Q_PER_KV * HEAD_DIM), lambda b, g, r: (b * nqb + r, g)),
        out_shape=jax.ShapeDtypeStruct((batch * seq, n_q * HEAD_DIM), BF16),
        compiler_params=_params("parallel", "parallel", "arbitrary"),
        name="attention",
    )(*args)


def _dft_mats(n):
    k = jnp.arange(n, dtype=jnp.int32)
    ang = ((k[:, None] * k[None, :]) % n).astype(F32) * (2.0 * math.pi / n)
    return jnp.cos(ang).astype(BF16), jnp.sin(ang).astype(BF16)


def _chan_dft_kernel(a_ref, cd_ref, sd_ref, pq_ref, *, groups):
    aw = a_ref.shape[1]
    dg = aw // groups
    for g in range(groups):
        ag = a_ref[:, g * dg:(g + 1) * dg]
        pq_ref[:, g * dg:(g + 1) * dg] = _dot(ag, cd_ref[...]).astype(BF16)
        pq_ref[:, aw + g * dg:aw + (g + 1) * dg] = _dot(ag, sd_ref[...]).astype(BF16)


def _chan_dft(z, aw):
    t = z.shape[0]
    dg = aw // A_GROUPS
    bm = _pick(t, 1024)
    cd, sd = _dft_mats(dg)
    full = pl.BlockSpec((dg, dg), lambda i: (0, 0))
    return pl.pallas_call(
        functools.partial(_chan_dft_kernel, groups=A_GROUPS),
        grid=(t // bm,),
        in_specs=[pl.BlockSpec((bm, aw), lambda i: (i, 0)), full, full],
        out_specs=pl.BlockSpec((bm, 2 * aw), lambda i: (i, 0)),
        out_shape=jax.ShapeDtypeStruct((t, 2 * aw), BF16),
        compiler_params=_params("parallel"),
        name="chan_dft",
    )(z, cd, sd)


def _seq_dft_kernel(pq_ref, cl_ref, sl_ref, fw_ref, fb_ref, y_ref, *, groups, norm):
    aw = y_ref.shape[1]
    dg = aw // groups
    f = (_dot(cl_ref[...], pq_ref[:, :aw]) - _dot(sl_ref[...], pq_ref[:, aw:])) * norm
    fb = fb_ref[...]
    for g in range(groups):
        sl = slice(g * dg, (g + 1) * dg)
        y_ref[:, sl] = (_dot(f[:, sl].astype(BF16), fw_ref[g]) + fb[:, sl]).astype(BF16)


def _seq_dft(pq, batch, seq, fw, fb):
    aw = pq.shape[1] // 2
    dg = aw // A_GROUPS
    bl = _pick(seq, 512)
    nlb = seq // bl
    cl, sl = _dft_mats(seq)
    mat = pl.BlockSpec((bl, seq), lambda b, r: (r, 0))
    return pl.pallas_call(
        functools.partial(_seq_dft_kernel, groups=A_GROUPS, norm=1.0 / math.sqrt(seq * dg)),
        grid=(batch, nlb),
        in_specs=[pl.BlockSpec((seq, 2 * aw), lambda b, r: (b, 0), pipeline_mode=pl.Buffered(1)),
                  mat, mat,
                  pl.BlockSpec((A_GROUPS, dg, dg), lambda b, r: (0, 0, 0)),
                  pl.BlockSpec((1, aw), lambda b, r: (0, 0))],
        out_specs=pl.BlockSpec((bl, aw), lambda b, r: (b * nlb + r, 0)),
        out_shape=jax.ShapeDtypeStruct((batch * seq, aw), BF16),
        compiler_params=_params("parallel", "arbitrary"),
        name="seq_dft",
    )(pq, cl, sl, fw, fb.reshape(1, aw))


def _ffn_up_kernel(hp_ref, h_ref, hn_ref, wg_ref, wv_ref, cwg_ref, cwv_ref, cbg_ref, cbv_ref,
                   pcwg_ref, pcwv_ref, pcbg_ref, pcbv_ref, ue_ref, uo_ref, ul_ref, hh_ref, zg_ref, zv_ref,
                   *, bm, seq, sub, ni, nj):
    s = pl.program_id(0)
    i = s // nj
    halo = CONV_HALO
    rows = bm + 2 * halo
    interior = seq < bm
    first, second = slice(0, sub), slice(sub, 2 * sub)

    @pl.when(s == 0)
    def _():
        zg_ref[...] = jnp.zeros_like(zg_ref)
        zv_ref[...] = jnp.zeros_like(zv_ref)

    @pl.when(s % nj == 0)
    def _():
        lo_ok = (i * bm) % seq != 0
        hi_ok = ((i + 1) * bm) % seq != 0
        hh_ref[0:halo] = jnp.where(lo_ok, hp_ref[...], jnp.zeros_like(hp_ref))
        hh_ref[halo:halo + bm] = h_ref[...]
        hh_ref[halo + bm:rows] = jnp.where(hi_ok, hn_ref[...], jnp.zeros_like(hn_ref))

    hh = hh_ref[...]
    if interior:
        pos = lax.broadcasted_iota(jnp.int32, (bm, 1), 0) % seq
        not_first = (pos != 0).astype(F32)
        not_last = (pos != seq - 1).astype(F32)

    def conv(z, cw_ref, cb_ref, cols):
        prev = pltpu.roll(z, 1, axis=0)[halo:halo + bm]
        nxt = pltpu.roll(z, rows - 1, axis=0)[halo:halo + bm]
        if interior:
            prev, nxt = prev * not_first, nxt * not_last
        return (prev * cw_ref[0:1, cols] + z[halo:halo + bm] * cw_ref[1:2, cols]
                + nxt * cw_ref[2:3, cols] + cb_ref[:, cols])

    def gate(zg, zv, cwg, cwv, cbg, cbv, cols):
        g = conv(zg, cwg, cbg, cols)
        v = conv(zv, cwv, cbv, cols)
        return (g * jax.nn.sigmoid(g) * v).astype(BF16)

    uo_ref[...] = gate(zg_ref[...], zv_ref[...], pcwg_ref, pcwv_ref, pcbg_ref, pcbv_ref, second)
    ue_ref[...] = gate(_dot(hh, wg_ref[:, first]), _dot(hh, wv_ref[:, first]),
                       cwg_ref, cwv_ref, cbg_ref, cbv_ref, first)
    zg_ref[...] = _dot(hh, wg_ref[:, second])
    zv_ref[...] = _dot(hh, wv_ref[:, second])

    @pl.when(s == ni * nj - 1)
    def _():
        ul_ref[...] = gate(zg_ref[...], zv_ref[...], cwg_ref, cwv_ref, cbg_ref, cbv_ref, second)


def _ffn_up(h, w_up, conv_w, conv_b, stream):
    t, d = h.shape
    f = w_up.shape[1] // 2
    bm = stream.row_block(t, 1024)
    bn = FFN_COLS
    sub = bn // 2
    ni, nj = t // bm, f // bn
    halo = CONV_HALO
    hb = bm // halo
    last = t // halo - 1
    assert f % bn == 0 and bm % halo == 0 and conv_w.shape[0] == 3

    def row(s):
        return s // nj

    def prev(s):
        return jnp.maximum(s - 1, 0)

    def cur(off):
        return lambda s: (0, s % nj + off)

    def old(off):
        return lambda s: (0, prev(s) % nj + off)

    half = jax.ShapeDtypeStruct((t, f // 2), BF16)
    u_first, u_second, u_last = pl.pallas_call(
        functools.partial(_ffn_up_kernel, bm=bm, seq=stream.seq, sub=sub, ni=ni, nj=nj),
        grid=(ni * nj,),
        in_specs=[pl.BlockSpec((halo, d), lambda s: (jnp.maximum(row(s) * hb - 1, 0), 0)),
                  pl.BlockSpec((bm, d), lambda s: (row(s), 0), pipeline_mode=pl.Buffered(1)),
                  pl.BlockSpec((halo, d), lambda s: (jnp.minimum((row(s) + 1) * hb, last), 0)),
                  pl.BlockSpec((d, bn), cur(0)), pl.BlockSpec((d, bn), cur(nj)),
                  pl.BlockSpec((3, bn), cur(0)), pl.BlockSpec((3, bn), cur(nj)),
                  pl.BlockSpec((1, bn), cur(0)), pl.BlockSpec((1, bn), cur(nj)),
                  pl.BlockSpec((3, bn), old(0)), pl.BlockSpec((3, bn), old(nj)),
                  pl.BlockSpec((1, bn), old(0)), pl.BlockSpec((1, bn), old(nj))],
        out_specs=[pl.BlockSpec((bm, sub), lambda s: (row(s), s % nj)),
                   pl.BlockSpec((bm, sub), lambda s: (prev(s) // nj, prev(s) % nj)),
                   pl.BlockSpec((bm, sub), lambda s: (0, 0))],
        out_shape=[half, half, jax.ShapeDtypeStruct((bm, sub), BF16)],
        scratch_shapes=[pltpu.VMEM((bm + 2 * halo, d), BF16),
                        pltpu.VMEM((bm + 2 * halo, sub), F32),
                        pltpu.VMEM((bm + 2 * halo, sub), F32)],
        compiler_params=_params("arbitrary"),
        name="ffn_up",
    )(h, h, h, w_up, w_up, conv_w, conv_w, conv_b, conv_b, conv_w, conv_w, conv_b, conv_b)
    return u_first, u_second.at[(ni - 1) * bm:, (nj - 1) * sub:].set(u_last)


def _pad_ffn(w_up, conv_w, conv_b, w_down):
    d, f2 = w_up.shape
    f = f2 // 2
    pad = -f % FFN_COLS
    fp = f + pad
    w_up = jnp.pad(w_up.reshape(d, 2, f).astype(BF16), ((0, 0), (0, 0), (0, pad))).reshape(d, 2 * fp)
    conv_w = jnp.pad(conv_w.reshape(-1, 2, f), ((0, 0), (0, 0), (0, pad))).reshape(-1, 2 * fp)
    conv_b = jnp.pad(conv_b.reshape(1, 2, f), ((0, 0), (0, 0), (0, pad))).reshape(1, 2 * fp)
    w_down = jnp.pad(w_down.astype(BF16), ((0, pad), (0, 0))).reshape(fp // FFN_COLS, 2, FFN_COLS // 2, -1)
    return w_up, conv_w, conv_b, w_down[:, 0].reshape(fp // 2, -1), w_down[:, 1].reshape(fp // 2, -1)


def _gmlp_mix_kernel(u_ref, v_ref, g_ref, b_ref, ws_ref, bst_ref, o_ref, *, groups):
    bm, c = v_ref.shape
    cg = c // groups
    vn = (_standardize(v_ref[...].astype(F32)) * g_ref[...] + b_ref[...]).astype(BF16)
    for ch in range(bm // CHUNK):
        rows = slice(ch * CHUNK, (ch + 1) * CHUNK)
        for g in range(groups):
            cols = slice(g * cg, (g + 1) * cg)
            mixed = _dot(ws_ref[g], vn[rows, cols]) + bst_ref[:, g:g + 1]
            o_ref[rows, cols] = (u_ref[rows, cols].astype(F32) * mixed).astype(BF16)


def _gmlp_mix(uv, v_g, v_b, ws, bs, stream):
    t, c2 = uv.shape
    c = c2 // 2
    bm = _pick(stream.seq, 256)
    assert bm % CHUNK == 0
    vec = pl.BlockSpec((1, c), lambda i: (0, 0))
    return pl.pallas_call(
        functools.partial(_gmlp_mix_kernel, groups=C_GROUPS),
        grid=(t // bm,),
        in_specs=[pl.BlockSpec((bm, c), lambda i: (i, 0)),
                  pl.BlockSpec((bm, c), lambda i: (i, 1)),
                  vec, vec,
                  pl.BlockSpec((C_GROUPS, CHUNK, CHUNK), lambda i: (0, 0, 0)),
                  pl.BlockSpec((CHUNK, C_GROUPS), lambda i: (0, 0))],
        out_specs=pl.BlockSpec((bm, c), lambda i: (i, 0)),
        out_shape=jax.ShapeDtypeStruct((t, c), BF16),
        compiler_params=_params("parallel"),
        name="gmlp_mix",
    )(uv, uv, v_g.reshape(1, c), v_b.reshape(1, c), ws, bs.T)


def kernel(x_prompt, x_sample, cache_k, cache_v, c, c_ctx, w_mod, b_mod, ln_g, ln_b, attn_w_in, fourier_w, fourier_b, q_norm_g, k_norm_g, attn_w_out, gmlp_w_in, gmlp_v_ln_g, gmlp_v_ln_b, gmlp_ws, gmlp_bs, gmlp_w_out, ffn_w_up, ffn_conv_w, ffn_conv_b, ffn_w_down):
    batch, seq, d = x_prompt.shape
    dec_batch, dec_seq, _ = x_sample.shape
    depth = w_mod.shape[0]
    n_kv = (d // 4) // HEAD_DIM
    aw = d // 4
    alpha = (2 * depth) ** 0.25
    assert dec_batch + 1 <= COND_ROWS and dec_seq % GRID_W == 0

    streams = (_Stream(seq, 0, False), _Stream(dec_seq, 1, True))
    xs = [x_prompt.reshape(batch * seq, d), x_sample.reshape(dec_batch * dec_seq, d)]
    batches = (batch, dec_batch)

    cond = jnp.zeros((COND_ROWS, d), F32).at[0].set(c_ctx).at[1:1 + dec_batch].set(c)
    mods = _adaln(cond, w_mod, b_mod).reshape(depth, COND_ROWS * N_MOD, 1, d)
    rope_tabs = _rope_tables(dec_seq)
    past = cache_k.shape[2]
    new_k, new_v = [], []

    hs = [_modulate(xs[s], mods[0], streams[s], 0) for s in range(2)]
    for l in range(depth):
        i = l // 2
        mod = mods[l]
        ys = []
        if l % 2 == 0:
            w_in = attn_w_in[i].astype(BF16)
            w_out_a = attn_w_out[i, :aw].astype(BF16)
            w_out_o = attn_w_out[i, aw:].astype(BF16)
            fw = fourier_w[i].astype(BF16)
            fb = fourier_b[i].reshape(aw)
            ck = cache_k[:, i].reshape(dec_batch, past, n_kv * HEAD_DIM)
            cv = cache_v[:, i].reshape(dec_batch, past, n_kv * HEAD_DIM)
            for s in range(2):
                st = streams[s]
                if s == 0:
                    z, k32, v32 = _even_in(hs[s], w_in, q_norm_g[i], k_norm_g[i], st, state=True)
                    new_k.append(k32.reshape(batch, seq, n_kv, HEAD_DIM))
                    new_v.append(v32.reshape(batch, seq, n_kv, HEAD_DIM))
                    o = _attention(z, batch, seq, d)
                else:
                    (z,) = _even_in(hs[s], w_in, q_norm_g[i], k_norm_g[i], st, rope_tabs=rope_tabs)
                    o = _attention(z, dec_batch, dec_seq, d, ck, cv)
                ya = _seq_dft(_chan_dft(z, aw), batches[s], st.seq, fw, fb)
                ys.append(_mm([(ya, w_out_a), (o, w_out_o)], 1024, 1024, name="even_out"))
        else:
            w_in = gmlp_w_in[i].astype(BF16)
            w_out = gmlp_w_out[i].astype(BF16)
            ws = gmlp_ws[i].astype(BF16)
            for s in range(2):
                uv = _mm([(hs[s], w_in)], 1024, 1024, out_dtype=BF16, gelu=True, name="gmlp_in")
                gated = _gmlp_mix(uv, gmlp_v_ln_g[i], gmlp_v_ln_b[i], ws, gmlp_bs[i], streams[s])
                ys.append(_mm([(gated, w_out)], 1024, 1024, name="gmlp_out"))

        w_up, conv_w, conv_b, w_down_first, w_down_second = _pad_ffn(ffn_w_up[l], ffn_conv_w[l], ffn_conv_b[l], ffn_w_down[l])
        for s in range(2):
            st = streams[s]
            xs[s], h2 = _resnorm(xs[s], ys[s], mod, st, 2, ln_g[l, 0], ln_b[l, 0], alpha, mod, 3)
            u_first, u_second = _ffn_up(h2, w_up, conv_w, conv_b, st)
            f = _mm([(u_first, w_down_first), (u_second, w_down_second)], 512, 512, name="ffn_down")
            nxt = mods[l + 1] if l + 1 < depth else None
            xs[s], hs[s] = _resnorm(xs[s], f, mod, st, 5, ln_g[l, 1], ln_b[l, 1], alpha, nxt, 0)

    state_k = jnp.stack(new_k, axis=1)
    state_v = jnp.stack(new_v, axis=1)
    return (xs[0].reshape(batch, seq, d), xs[1].reshape(dec_batch, dec_seq, d), state_k, state_v)
```

```python
import functools
import math
from typing import NamedTuple

import jax
import jax.numpy as jnp
from jax import lax
from jax.experimental import pallas as pl
from jax.experimental.pallas import tpu as pltpu

F32 = jnp.float32
BF16 = jnp.bfloat16

HEAD_DIM = 128
GRID_W = 64
CHUNK = 128
A_GROUPS = 4
C_GROUPS = 8
Q_PER_KV = 3
N_MOD = 6
ROPE_BASE = 10000.0
LN_EPS = 1e-6
COND_ROWS = 16
CONV_HALO = 16
EVEN_IN_COLS = 512
FFN_COLS = 256
ATTN_CHAIN_ROWS = 256
V7X_VMEM_LIMIT_BYTES = 56 * 1024 * 1024

class _Stream(NamedTuple):
    seq: int
    cond_base: int
    per_seq: bool

    def cond_row(self, i, bm):
        return self.cond_base + ((i * bm) // self.seq if self.per_seq else 0)

    def row_block(self, tokens, pref):
        full = self.seq if self.per_seq else tokens
        b = min(full, pref)
        assert full % b == 0 and (b % self.seq == 0 or self.seq % b == 0)
        return b


def _pick(dim, pref):
    b = min(dim, pref)
    assert dim % b == 0, (dim, pref)
    return b


def _params(*sem):
    return pltpu.CompilerParams(dimension_semantics=sem, vmem_limit_bytes=V7X_VMEM_LIMIT_BYTES)


def _standardize(x):
    xc = x - jnp.mean(x, axis=-1, keepdims=True)
    return xc * lax.rsqrt(jnp.mean(xc * xc, axis=-1, keepdims=True) + LN_EPS)


def _dot(a, b):
    return jnp.dot(a, b, preferred_element_type=F32)


def _mod_spec(stream, bm, d, j):
    return pl.BlockSpec((None, 1, d), lambda i, *_: (stream.cond_row(i, bm) * N_MOD + j, 0, 0))


def _adaln_kernel(c_ref, w_ref, b_ref, o_ref):
    c = c_ref[...]
    s = (c * jax.nn.sigmoid(c)).astype(BF16)
    o_ref[...] = _dot(s, w_ref[...].astype(BF16)) + b_ref[...]


def _adaln(cond, w_mod, b_mod):
    depth, d, n = w_mod.shape
    rows = cond.shape[0]
    bn = _pick(n, 512)
    return pl.pallas_call(
        _adaln_kernel,
        grid=(depth, n // bn),
        in_specs=[pl.BlockSpec((rows, d), lambda l, j: (0, 0)),
                  pl.BlockSpec((None, d, bn), lambda l, j: (l, 0, j)),
                  pl.BlockSpec((None, 1, bn), lambda l, j: (l, 0, j))],
        out_specs=pl.BlockSpec((None, rows, bn), lambda l, j: (l, 0, j)),
        out_shape=jax.ShapeDtypeStruct((depth, rows, n), F32),
        compiler_params=_params("parallel", "parallel"),
        name="adaln",
    )(cond, w_mod, b_mod.reshape(depth, 1, n))


def _modulate_kernel(x_ref, sh_ref, sc_ref, h_ref):
    h_ref[...] = (_standardize(x_ref[...]) * (1.0 + sc_ref[...]) + sh_ref[...]).astype(BF16)


def _modulate(x, mod, stream, j_shift):
    t, d = x.shape
    bm = stream.row_block(t, 256)
    row = pl.BlockSpec((bm, d), lambda i: (i, 0))
    return pl.pallas_call(
        _modulate_kernel,
        grid=(t // bm,),
        in_specs=[row, _mod_spec(stream, bm, d, j_shift), _mod_spec(stream, bm, d, j_shift + 1)],
        out_specs=row,
        out_shape=jax.ShapeDtypeStruct((t, d), BF16),
        compiler_params=_params("parallel"),
        name="modulate",
    )(x, mod, mod)


def _resnorm_kernel(*refs, alpha, with_mod):
    if with_mod:
        x_ref, y_ref, gate_ref, g_ref, b_ref, sh_ref, sc_ref, xo_ref, h_ref = refs
    else:
        x_ref, y_ref, gate_ref, g_ref, b_ref, xo_ref = refs
    r = alpha * x_ref[...] + gate_ref[...] * y_ref[...]
    xn = _standardize(r) * g_ref[...] + b_ref[...]
    xo_ref[...] = xn
    if with_mod:
        h_ref[...] = (_standardize(xn) * (1.0 + sc_ref[...]) + sh_ref[...]).astype(BF16)


def _resnorm(x, y, mod, stream, j_gate, ln_g, ln_b, alpha, next_mod=None, j_next=None):
    t, d = x.shape
    bm = stream.row_block(t, 256)
    row = pl.BlockSpec((bm, d), lambda i: (i, 0))
    vec = pl.BlockSpec((1, d), lambda i: (0, 0))
    with_mod = next_mod is not None
    in_specs = [row, row, _mod_spec(stream, bm, d, j_gate), vec, vec]
    args = [x, y, mod, ln_g.reshape(1, d), ln_b.reshape(1, d)]
    out_specs = [row]
    out_shape = [jax.ShapeDtypeStruct((t, d), F32)]
    if with_mod:
        in_specs += [_mod_spec(stream, bm, d, j_next), _mod_spec(stream, bm, d, j_next + 1)]
        args += [next_mod, next_mod]
        out_specs.append(row)
        out_shape.append(jax.ShapeDtypeStruct((t, d), BF16))
    out = pl.pallas_call(
        functools.partial(_resnorm_kernel, alpha=alpha, with_mod=with_mod),
        grid=(t // bm,),
        in_specs=in_specs,
        out_specs=out_specs,
        out_shape=out_shape,
        compiler_params=_params("parallel"),
        name="resnorm",
    )(*args)
    return (out[0], out[1]) if with_mod else (out[0], None)


def _gelu_tanh(x):
    return x * (0.5 * (1.0 + jnp.tanh(math.sqrt(2.0 / math.pi) * (x + 0.044715 * (x * x * x)))))


def _mm_kernel(*refs, n_pairs, gelu):
    o_ref = refs[2 * n_pairs]
    acc = _dot(refs[0][...], refs[n_pairs][...])
    for p in range(1, n_pairs):
        acc = acc + _dot(refs[p][...], refs[n_pairs + p][...])
    if gelu:
        acc = _gelu_tanh(acc)
    o_ref[...] = acc.astype(o_ref.dtype)


def _mm(pairs, bm, bn, out_dtype=F32, gelu=False, name="mm"):
    pairs = [p if len(p) == 3 else (*p, 0) for p in pairs]
    m = pairs[0][0].shape[0]
    n = pairs[0][1].shape[1]
    bm, bn = _pick(m, bm), _pick(n, bn)
    assert all(w.shape[0] % x.shape[1] == 0 for x, w, _ in pairs)
    x_specs = [pl.BlockSpec((bm, x.shape[1]), lambda i, j: (i, 0)) for x, _, _ in pairs]
    w_specs = [pl.BlockSpec((x.shape[1], bn), lambda i, j, kb=kb: (kb, j)) for x, _, kb in pairs]
    return pl.pallas_call(
        functools.partial(_mm_kernel, n_pairs=len(pairs), gelu=gelu),
        grid=(m // bm, n // bn),
        in_specs=x_specs + w_specs,
        out_specs=pl.BlockSpec((bm, bn), lambda i, j: (i, j)),
        out_shape=jax.ShapeDtypeStruct((m, n), out_dtype),
        compiler_params=_params("parallel", "parallel"),
        name=name,
    )(*[x for x, _, _ in pairs], *[w for _, w, _ in pairs])


def _head_rms(z, gain):
    return z * lax.rsqrt(jnp.mean(z * z, axis=-1, keepdims=True) + LN_EPS) * gain


def _rope(z, c_ref, sa_ref, sb_ref):
    hd = z.shape[-1]
    return (z * c_ref[...] + pltpu.roll(z, hd - hd // 4, axis=1) * sa_ref[...]
            + pltpu.roll(z, hd // 4, axis=1) * sb_ref[...])


def _even_in_kernel(*refs, heads_per_block, nb, rope, state, q_scale):
    k_lo, v_lo = (1 + Q_PER_KV) * nb, (2 + Q_PER_KV) * nb
    h_ref, w_ref, qg_ref, kg_ref = refs[:4]
    refs = refs[4:]
    if rope:
        c_ref, sa_ref, sb_ref = refs[:3]
        refs = refs[3:]
    z_ref = refs[0]
    if state:
        k32_ref, v32_ref = refs[1:3]
    j = pl.program_id(1)
    z = _dot(h_ref[...], w_ref[...])

    def heads(gain_ref, scale, k32=None):
        for hh in range(heads_per_block):
            sl = slice(hh * HEAD_DIM, (hh + 1) * HEAD_DIM)
            n = _head_rms(z[:, sl], gain_ref[...])
            if k32 is not None:
                k32[:, sl] = n
            if rope:
                n = _rope(n, c_ref, sa_ref, sb_ref)
            if scale != 1.0:
                n = n * scale
            z_ref[:, sl] = n.astype(BF16)

    @pl.when(jnp.logical_or(j < nb, j >= v_lo))
    def _():
        z_ref[...] = z.astype(BF16)

    @pl.when(jnp.logical_and(j >= nb, j < k_lo))
    def _():
        heads(qg_ref, q_scale)

    @pl.when(jnp.logical_and(j >= k_lo, j < v_lo))
    def _():
        heads(kg_ref, 1.0, k32_ref if state else None)

    if state:
        @pl.when(j >= v_lo)
        def _():
            v32_ref[...] = z


def _even_in(h, w_in, q_g, k_g, stream, rope_tabs=None, state=False):
    t, d = h.shape
    n = w_in.shape[1]
    quarter = d // 4
    assert n == (3 + Q_PER_KV) * quarter
    bn = _pick(quarter, EVEN_IN_COLS)
    nb = quarter // bn
    k_lo, v_lo = (1 + Q_PER_KV) * nb, (2 + Q_PER_KV) * nb
    bm = stream.row_block(t, 1024)
    vec = pl.BlockSpec((1, HEAD_DIM), lambda i, j: (0, 0))
    in_specs = [pl.BlockSpec((bm, d), lambda i, j: (i, 0)),
                pl.BlockSpec((d, bn), lambda i, j: (0, j)), vec, vec]
    args = [h, w_in, q_g.reshape(1, HEAD_DIM), k_g.reshape(1, HEAD_DIM)]
    if rope_tabs is not None:
        blocks_per_seq = stream.seq // bm
        tab = pl.BlockSpec((bm, HEAD_DIM), lambda i, j: (i % blocks_per_seq, 0))
        in_specs += [tab, tab, tab]
        args += list(rope_tabs)
    out_specs = [pl.BlockSpec((bm, bn), lambda i, j: (i, j))]
    out_shape = [jax.ShapeDtypeStruct((t, n), BF16)]
    if state:
        out_specs += [pl.BlockSpec((bm, bn), lambda i, j: (i, jnp.clip(j - k_lo, 0, nb - 1))),
                      pl.BlockSpec((bm, bn), lambda i, j: (i, jnp.clip(j - v_lo, 0, nb - 1)))]
        out_shape += [jax.ShapeDtypeStruct((t, quarter), F32)] * 2
    return pl.pallas_call(
        functools.partial(_even_in_kernel, heads_per_block=bn // HEAD_DIM, nb=nb, rope=rope_tabs is not None,
                          state=state, q_scale=HEAD_DIM ** -0.5),
        grid=(t // bm, n // bn),
        in_specs=in_specs,
        out_specs=out_specs,
        out_shape=out_shape,
        compiler_params=_params("parallel", "arbitrary"),
        name="even_in",
    )(*args)


def _rope_tables(seq):
    rows = (jnp.arange(seq) // GRID_W).astype(F32)
    cols = (jnp.arange(seq) % GRID_W).astype(F32)
    nf = HEAD_DIM // 4
    inv = ROPE_BASE ** (-jnp.arange(nf, dtype=F32) / nf)
    lane = jnp.arange(HEAD_DIM)
    pos = jnp.where((lane // (2 * nf))[None, :] == 0, rows[:, None], cols[:, None])
    ang = pos * inv[lane % nf][None, :]
    first = ((lane // nf) % 2 == 0)[None, :]
    c, s = jnp.cos(ang), jnp.sin(ang)
    return c, jnp.where(first, -s, 0.0), jnp.where(first, 0.0, s)


def _attn_kernel(*refs, cached, bq):
    q_refs = refs[:Q_PER_KV]
    k_ref, v_ref = refs[Q_PER_KV:Q_PER_KV + 2]
    if cached:
        ck_ref, cv_ref = refs[Q_PER_KV + 2:Q_PER_KV + 4]
    o_ref = refs[-1]
    dims = (((1,), (1,)), ((), ()))
    def with_ones(v):
        lane = lax.broadcasted_iota(jnp.int32, (v.shape[0], HEAD_DIM), 1)
        return jnp.concatenate([v, (lane == 0).astype(BF16)], axis=1)

    half = k_ref.shape[0] // 2
    ks = [k_ref[0:half], k_ref[half:]]
    vs = [with_ones(v_ref[0:half]), with_ones(v_ref[half:])]
    if cached:
        ks.append(ck_ref[...].astype(BF16))
        vs.append(with_ones(cv_ref[...].astype(BF16)))
    chunk = min(bq, ATTN_CHAIN_ROWS)
    for t in range(Q_PER_KV):
        for c in range(bq // chunk):
            rows = slice(c * chunk, (c + 1) * chunk)
            q = q_refs[t][rows, :]
            ss = [lax.dot_general(q, kk, dims, preferred_element_type=F32) for kk in ks]
            m = functools.reduce(jnp.maximum, [jnp.max(s, axis=-1, keepdims=True) for s in ss])
            acc = sum(_dot(jnp.exp((s - m).astype(BF16)), vv) for s, vv in zip(ss, vs))
            o = acc[:, :HEAD_DIM] / acc[:, HEAD_DIM:HEAD_DIM + 1]
            o_ref[rows, t * HEAD_DIM:(t + 1) * HEAD_DIM] = o.astype(BF16)


def _attention(z, batch, seq, d, cache_k=None, cache_v=None):
    a_blocks = (d // 4) // HEAD_DIM
    n_kv = (d // 4) // HEAD_DIM
    n_q = Q_PER_KV * n_kv
    bq = _pick(seq, 1024)
    nqb = seq // bq
    cached = cache_k is not None

    def q_spec(t):
        return pl.BlockSpec((bq, HEAD_DIM), lambda b, g, r: (b * nqb + r, a_blocks + Q_PER_KV * g + t))

    in_specs = [q_spec(t) for t in range(Q_PER_KV)]
    in_specs += [pl.BlockSpec((seq, HEAD_DIM), lambda b, g, r: (b, a_blocks + n_q + g)),
                 pl.BlockSpec((seq, HEAD_DIM), lambda b, g, r: (b, a_blocks + n_q + n_kv + g))]
    args = [z] * (Q_PER_KV + 2)
    if cached:
        past = cache_k.shape[1]
        cspec = pl.BlockSpec((None, past, HEAD_DIM), lambda b, g, r: (b, 0, g))
        in_specs += [cspec, cspec]
        args += [cache_k, cache_v]
    return pl.pallas_call(
        functools.partial(_attn_kernel, cached=cached, bq=bq),
        grid=(batch, n_kv, nqb),
        in_specs=in_specs,
        out_specs=pl.BlockSpec((bq, Q_PER_KV * HEAD_DIM), lambda b, g, r: (b * nqb + r, g)),
        out_shape=jax.ShapeDtypeStruct((batch * seq, n_q * HEAD_DIM), BF16),
        compiler_params=_params("parallel", "parallel", "arbitrary"),
        name="attention",
    )(*args)


def _dft_mats(n):
    m = jnp.arange(n, dtype=jnp.int32)

    def tables(k, period):
        ang = ((k[:, None] * m[None, :]) % period).astype(F32) * (2.0 * math.pi / period)
        return jnp.cos(ang), jnp.sin(ang)

    r = math.isqrt(n)
    if r * r != n:
        c, s = tables(m, n)
        return c.astype(BF16), s.astype(BF16)
    kr = jnp.arange(r, dtype=jnp.int32)
    c1, s1 = tables(kr, r)
    c2, s2 = tables(kr, n)
    c = c1[:, None, :] * c2[None, :, :] - s1[:, None, :] * s2[None, :, :]
    s = s1[:, None, :] * c2[None, :, :] + c1[:, None, :] * s2[None, :, :]
    return c.reshape(n, n).astype(BF16), s.reshape(n, n).astype(BF16)


def _chan_dft_kernel(a_ref, cd_ref, sd_ref, pq_ref, *, groups):
    aw = a_ref.shape[1]
    dg = aw // groups
    for g in range(groups):
        ag = a_ref[:, g * dg:(g + 1) * dg]
        pq_ref[:, g * dg:(g + 1) * dg] = _dot(ag, cd_ref[...]).astype(BF16)
        pq_ref[:, aw + g * dg:aw + (g + 1) * dg] = _dot(ag, sd_ref[...]).astype(BF16)


def _chan_dft(z, aw):
    t = z.shape[0]
    dg = aw // A_GROUPS
    bm = _pick(t, 1024)
    cd, sd = _dft_mats(dg)
    full = pl.BlockSpec((dg, dg), lambda i: (0, 0))
    return pl.pallas_call(
        functools.partial(_chan_dft_kernel, groups=A_GROUPS),
        grid=(t // bm,),
        in_specs=[pl.BlockSpec((bm, aw), lambda i: (i, 0)), full, full],
        out_specs=pl.BlockSpec((bm, 2 * aw), lambda i: (i, 0)),
        out_shape=jax.ShapeDtypeStruct((t, 2 * aw), BF16),
        compiler_params=_params("parallel"),
        name="chan_dft",
    )(z, cd, sd)


def _seq_dft_kernel(pq_ref, cl_ref, sl_ref, fw_ref, fb_ref, y_ref, *, groups, norm):
    aw = y_ref.shape[1]
    dg = aw // groups
    f = (_dot(cl_ref[...], pq_ref[:, :aw]) - _dot(sl_ref[...], pq_ref[:, aw:])) * norm
    fb = fb_ref[...]
    for g in range(groups):
        sl = slice(g * dg, (g + 1) * dg)
        y_ref[:, sl] = (_dot(f[:, sl].astype(BF16), fw_ref[g]) + fb[:, sl]).astype(BF16)


def _seq_dft(pq, batch, seq, fw, fb):
    aw = pq.shape[1] // 2
    dg = aw // A_GROUPS
    bl = _pick(seq, 512)
    nlb = seq // bl
    cl, sl = _dft_mats(seq)
    mat = pl.BlockSpec((bl, seq), lambda b, r: (r, 0))
    return pl.pallas_call(
        functools.partial(_seq_dft_kernel, groups=A_GROUPS, norm=1.0 / math.sqrt(seq * dg)),
        grid=(batch, nlb),
        in_specs=[pl.BlockSpec((seq, 2 * aw), lambda b, r: (b, 0), pipeline_mode=pl.Buffered(1)),
                  mat, mat,
                  pl.BlockSpec((A_GROUPS, dg, dg), lambda b, r: (0, 0, 0)),
                  pl.BlockSpec((1, aw), lambda b, r: (0, 0))],
        out_specs=pl.BlockSpec((bl, aw), lambda b, r: (b * nlb + r, 0)),
        out_shape=jax.ShapeDtypeStruct((batch * seq, aw), BF16),
        compiler_params=_params("parallel", "arbitrary"),
        name="seq_dft",
    )(pq, cl, sl, fw, fb.reshape(1, aw))


def _ffn_up_kernel(hp_ref, h_ref, hn_ref, wg1_ref, wg2_ref, wv1_ref, wv2_ref, cg1_ref, cg2_ref, cv1_ref, cv2_ref,
                   bg1_ref, bg2_ref, bv1_ref, bv2_ref, pcg_ref, pcv_ref, pbg_ref, pbv_ref,
                   ue_ref, uo_ref, ul_ref, hh_ref, zg_ref, zv_ref, *, bm, seq, ni, nj):
    s = pl.program_id(0)
    i = s // nj
    halo = CONV_HALO
    rows = bm + 2 * halo
    interior = seq < bm

    @pl.when(s == 0)
    def _():
        zg_ref[...] = jnp.zeros_like(zg_ref)
        zv_ref[...] = jnp.zeros_like(zv_ref)

    @pl.when(s % nj == 0)
    def _():
        lo_ok = (i * bm) % seq != 0
        hi_ok = ((i + 1) * bm) % seq != 0
        hh_ref[0:halo] = jnp.where(lo_ok, hp_ref[...], jnp.zeros_like(hp_ref))
        hh_ref[halo:halo + bm] = h_ref[...]
        hh_ref[halo + bm:rows] = jnp.where(hi_ok, hn_ref[...], jnp.zeros_like(hn_ref))

    hh = hh_ref[...]
    if interior:
        pos = lax.broadcasted_iota(jnp.int32, (bm, 1), 0) % seq
        not_first = (pos != 0).astype(F32)
        not_last = (pos != seq - 1).astype(F32)

    def conv(z, cw_ref, cb_ref):
        prev = pltpu.roll(z, 1, axis=0)[halo:halo + bm]
        nxt = pltpu.roll(z, rows - 1, axis=0)[halo:halo + bm]
        if interior:
            prev, nxt = prev * not_first, nxt * not_last
        return prev * cw_ref[0:1] + z[halo:halo + bm] * cw_ref[1:2] + nxt * cw_ref[2:3] + cb_ref[...]

    def gate(zg, zv, cwg, cwv, cbg, cbv):
        g = conv(zg, cwg, cbg)
        v = conv(zv, cwv, cbv)
        return (g * jax.nn.sigmoid(g) * v).astype(BF16)

    uo_ref[...] = gate(zg_ref[...], zv_ref[...], pcg_ref, pcv_ref, pbg_ref, pbv_ref)
    ue_ref[...] = gate(_dot(hh, wg1_ref[...]), _dot(hh, wv1_ref[...]), cg1_ref, cv1_ref, bg1_ref, bv1_ref)
    zg_ref[...] = _dot(hh, wg2_ref[...])
    zv_ref[...] = _dot(hh, wv2_ref[...])

    @pl.when(s == ni * nj - 1)
    def _():
        ul_ref[...] = gate(zg_ref[...], zv_ref[...], cg2_ref, cv2_ref, bg2_ref, bv2_ref)


def _ffn_up(h, w_up, conv_w, conv_b, stream):
    t, d = h.shape
    f = w_up.shape[1] // 2
    bm = stream.row_block(t, 1024)
    sub = FFN_COLS
    ni, nj = t // bm, f // (2 * sub)
    halo = CONV_HALO
    hb = bm // halo
    last = t // halo - 1
    assert f % (2 * sub) == 0 and bm % halo == 0 and conv_w.shape[0] == 3

    def row(s):
        return s // nj

    def prev(s):
        return jnp.maximum(s - 1, 0)

    def cur(rows, off):
        return pl.BlockSpec((rows, sub), lambda s: (0, s % nj + off * nj))

    def old(rows, off):
        return pl.BlockSpec((rows, sub), lambda s: (0, prev(s) % nj + off * nj))

    half = jax.ShapeDtypeStruct((t, f // 2), BF16)
    u_first, u_second, u_last = pl.pallas_call(
        functools.partial(_ffn_up_kernel, bm=bm, seq=stream.seq, ni=ni, nj=nj),
        grid=(ni * nj,),
        in_specs=[pl.BlockSpec((halo, d), lambda s: (jnp.maximum(row(s) * hb - 1, 0), 0)),
                  pl.BlockSpec((bm, d), lambda s: (row(s), 0), pipeline_mode=pl.Buffered(1)),
                  pl.BlockSpec((halo, d), lambda s: (jnp.minimum((row(s) + 1) * hb, last), 0)),
                  cur(d, 0), cur(d, 1), cur(d, 2), cur(d, 3),
                  cur(3, 0), cur(3, 1), cur(3, 2), cur(3, 3),
                  cur(1, 0), cur(1, 1), cur(1, 2), cur(1, 3),
                  old(3, 1), old(3, 3), old(1, 1), old(1, 3)],
        out_specs=[pl.BlockSpec((bm, sub), lambda s: (row(s), s % nj)),
                   pl.BlockSpec((bm, sub), lambda s: (prev(s) // nj, prev(s) % nj)),
                   pl.BlockSpec((bm, sub), lambda s: (0, 0))],
        out_shape=[half, half, jax.ShapeDtypeStruct((bm, sub), BF16)],
        scratch_shapes=[pltpu.VMEM((bm + 2 * halo, d), BF16),
                        pltpu.VMEM((bm + 2 * halo, sub), F32),
                        pltpu.VMEM((bm + 2 * halo, sub), F32)],
        compiler_params=_params("arbitrary"),
        name="ffn_up",
    )(h, h, h, *[w_up] * 4, *[conv_w] * 4, *[conv_b] * 4, conv_w, conv_w, conv_b, conv_b)
    return u_first, u_second.at[(ni - 1) * bm:, (nj - 1) * sub:].set(u_last)


def _pad_ffn(w_up, conv_w, conv_b, w_down):
    f = w_down.shape[0]
    pad = -f % (2 * FFN_COLS)

    def halves(a, dtype):
        a = a.astype(dtype)
        zeros = jnp.zeros((a.shape[0], pad), dtype)
        return jnp.concatenate([a[:, :f], zeros, a[:, f:], zeros], axis=1)

    return (halves(w_up, BF16), halves(conv_w, F32), halves(conv_b.reshape(1, 2 * f), F32),
            jnp.pad(w_down.astype(BF16), ((0, pad), (0, 0))))


def _gmlp_mix_kernel(u_ref, v_ref, g_ref, b_ref, ws_ref, bst_ref, o_ref, *, groups):
    bm, c = v_ref.shape
    cg = c // groups
    vn = (_standardize(v_ref[...].astype(F32)) * g_ref[...] + b_ref[...]).astype(BF16)
    for ch in range(bm // CHUNK):
        rows = slice(ch * CHUNK, (ch + 1) * CHUNK)
        for g in range(groups):
            cols = slice(g * cg, (g + 1) * cg)
            mixed = _dot(ws_ref[g], vn[rows, cols]) + bst_ref[:, g:g + 1]
            o_ref[rows, cols] = (u_ref[rows, cols].astype(F32) * mixed).astype(BF16)


def _gmlp_mix(uv, v_g, v_b, ws, bs, stream):
    t, c2 = uv.shape
    c = c2 // 2
    bm = _pick(stream.seq, 256)
    assert bm % CHUNK == 0
    vec = pl.BlockSpec((1, c), lambda i: (0, 0))
    return pl.pallas_call(
        functools.partial(_gmlp_mix_kernel, groups=C_GROUPS),
        grid=(t // bm,),
        in_specs=[pl.BlockSpec((bm, c), lambda i: (i, 0)),
                  pl.BlockSpec((bm, c), lambda i: (i, 1)),
                  vec, vec,
                  pl.BlockSpec((C_GROUPS, CHUNK, CHUNK), lambda i: (0, 0, 0)),
                  pl.BlockSpec((CHUNK, C_GROUPS), lambda i: (0, 0))],
        out_specs=pl.BlockSpec((bm, c), lambda i: (i, 0)),
        out_shape=jax.ShapeDtypeStruct((t, c), BF16),
        compiler_params=_params("parallel"),
        name="gmlp_mix",
    )(uv, uv, v_g.reshape(1, c), v_b.reshape(1, c), ws, bs.T)


def kernel(x_prompt, x_sample, cache_k, cache_v, c, c_ctx, w_mod, b_mod, ln_g, ln_b, attn_w_in, fourier_w, fourier_b, q_norm_g, k_norm_g, attn_w_out, gmlp_w_in, gmlp_v_ln_g, gmlp_v_ln_b, gmlp_ws, gmlp_bs, gmlp_w_out, ffn_w_up, ffn_conv_w, ffn_conv_b, ffn_w_down):
    batch, seq, d = x_prompt.shape
    dec_batch, dec_seq, _ = x_sample.shape
    depth = w_mod.shape[0]
    n_kv = (d // 4) // HEAD_DIM
    aw = d // 4
    alpha = (2 * depth) ** 0.25
    assert dec_batch + 1 <= COND_ROWS and dec_seq % GRID_W == 0

    streams = (_Stream(seq, 0, False), _Stream(dec_seq, 1, True))
    xs = [x_prompt.reshape(batch * seq, d), x_sample.reshape(dec_batch * dec_seq, d)]
    batches = (batch, dec_batch)

    cond = jnp.zeros((COND_ROWS, d), F32).at[0].set(c_ctx).at[1:1 + dec_batch].set(c)
    mods = _adaln(cond, w_mod, b_mod).reshape(depth, COND_ROWS * N_MOD, 1, d)
    rope_tabs = _rope_tables(dec_seq)
    past = cache_k.shape[2]
    new_k, new_v = [], []

    hs = [_modulate(xs[s], mods[0], streams[s], 0) for s in range(2)]
    for l in range(depth):
        i = l // 2
        mod = mods[l]
        ys = []
        if l % 2 == 0:
            w_in = attn_w_in[i].astype(BF16)
            w_out_a = attn_w_out[i, :aw].astype(BF16)
            w_out_o = attn_w_out[i, aw:].astype(BF16)
            fw = fourier_w[i].astype(BF16)
            fb = fourier_b[i].reshape(aw)
            ck = cache_k[:, i].reshape(dec_batch, past, n_kv * HEAD_DIM)
            cv = cache_v[:, i].reshape(dec_batch, past, n_kv * HEAD_DIM)
            for s in range(2):
                st = streams[s]
                if s == 0:
                    z, k32, v32 = _even_in(hs[s], w_in, q_norm_g[i], k_norm_g[i], st, state=True)
                    new_k.append(k32.reshape(batch, seq, n_kv, HEAD_DIM))
                    new_v.append(v32.reshape(batch, seq, n_kv, HEAD_DIM))
                    o = _attention(z, batch, seq, d)
                else:
                    (z,) = _even_in(hs[s], w_in, q_norm_g[i], k_norm_g[i], st, rope_tabs=rope_tabs)
                    o = _attention(z, dec_batch, dec_seq, d, ck, cv)
                ya = _seq_dft(_chan_dft(z, aw), batches[s], st.seq, fw, fb)
                ys.append(_mm([(ya, w_out_a), (o, w_out_o)], 1024, 1024, name="even_out"))
        else:
            w_in = gmlp_w_in[i].astype(BF16)
            w_out = gmlp_w_out[i].astype(BF16)
            ws = gmlp_ws[i].astype(BF16)
            for s in range(2):
                uv = _mm([(hs[s], w_in)], 1024, 1024, out_dtype=BF16, gelu=True, name="gmlp_in")
                gated = _gmlp_mix(uv, gmlp_v_ln_g[i], gmlp_v_ln_b[i], ws, gmlp_bs[i], streams[s])
                ys.append(_mm([(gated, w_out)], 1024, 1024, name="gmlp_out"))

        w_up, conv_w, conv_b, w_down = _pad_ffn(ffn_w_up[l], ffn_conv_w[l], ffn_conv_b[l], ffn_w_down[l])
        for s in range(2):
            st = streams[s]
            xs[s], h2 = _resnorm(xs[s], ys[s], mod, st, 2, ln_g[l, 0], ln_b[l, 0], alpha, mod, 3)
            u_first, u_second = _ffn_up(h2, w_up, conv_w, conv_b, st)
            f = _mm([(u_first, w_down, 0), (u_second, w_down, 1)], 512, 512, name="ffn_down")
            nxt = mods[l + 1] if l + 1 < depth else None
            xs[s], hs[s] = _resnorm(xs[s], f, mod, st, 5, ln_g[l, 1], ln_b[l, 1], alpha, nxt, 0)

    state_k = jnp.stack(new_k, axis=1)
    state_v = jnp.stack(new_v, axis=1)
    return (xs[0].reshape(batch, seq, d), xs[1].reshape(dec_batch, dec_seq, d), state_k, state_v)
```

```python
import functools
import math
from typing import NamedTuple

import jax
import jax.numpy as jnp
from jax import lax
from jax.experimental import pallas as pl
from jax.experimental.pallas import tpu as pltpu

F32 = jnp.float32
BF16 = jnp.bfloat16

HEAD_DIM = 128
GRID_W = 64
CHUNK = 128
A_GROUPS = 4
C_GROUPS = 8
Q_PER_KV = 3
N_MOD = 6
ROPE_BASE = 10000.0
LN_EPS = 1e-6
COND_ROWS = 16
CONV_HALO = 16
EVEN_IN_COLS = 512
FFN_COLS = 256
ATTN_CHAIN_ROWS = 256
V7X_VMEM_LIMIT_BYTES = 56 * 1024 * 1024

class _Stream(NamedTuple):
    seq: int
    cond_base: int
    per_seq: bool

    def cond_row(self, i, bm):
        return self.cond_base + ((i * bm) // self.seq if self.per_seq else 0)

    def row_block(self, tokens, pref):
        full = self.seq if self.per_seq else tokens
        b = min(full, pref)
        assert full % b == 0 and (b % self.seq == 0 or self.seq % b == 0)
        return b


def _pick(dim, pref):
    b = min(dim, pref)
    assert dim % b == 0, (dim, pref)
    return b


def _params(*sem):
    return pltpu.CompilerParams(dimension_semantics=sem, vmem_limit_bytes=V7X_VMEM_LIMIT_BYTES)


def _standardize(x):
    xc = x - jnp.mean(x, axis=-1, keepdims=True)
    return xc * lax.rsqrt(jnp.mean(xc * xc, axis=-1, keepdims=True) + LN_EPS)


def _dot(a, b):
    return jnp.dot(a, b, preferred_element_type=F32)


def _mod_spec(stream, bm, d, j):
    return pl.BlockSpec((None, 1, d), lambda i, *_: (stream.cond_row(i, bm) * N_MOD + j, 0, 0))


def _adaln_kernel(c_ref, w_ref, b_ref, o_ref):
    c = c_ref[...]
    s = (c * jax.nn.sigmoid(c)).astype(BF16)
    o_ref[...] = _dot(s, w_ref[...].astype(BF16)) + b_ref[...]


def _adaln(cond, w_mod, b_mod):
    depth, d, n = w_mod.shape
    rows = cond.shape[0]
    bn = _pick(n, 512)
    return pl.pallas_call(
        _adaln_kernel,
        grid=(depth, n // bn),
        in_specs=[pl.BlockSpec((rows, d), lambda l, j: (0, 0)),
                  pl.BlockSpec((None, d, bn), lambda l, j: (l, 0, j)),
                  pl.BlockSpec((None, 1, bn), lambda l, j: (l, 0, j))],
        out_specs=pl.BlockSpec((None, rows, bn), lambda l, j: (l, 0, j)),
        out_shape=jax.ShapeDtypeStruct((depth, rows, n), F32),
        compiler_params=_params("parallel", "parallel"),
        name="adaln",
    )(cond, w_mod, b_mod.reshape(depth, 1, n))


def _modulate_kernel(x_ref, sh_ref, sc_ref, h_ref):
    h_ref[...] = (_standardize(x_ref[...]) * (1.0 + sc_ref[...]) + sh_ref[...]).astype(BF16)


def _modulate(x, mod, stream, j_shift):
    t, d = x.shape
    bm = stream.row_block(t, 256)
    row = pl.BlockSpec((bm, d), lambda i: (i, 0))
    return pl.pallas_call(
        _modulate_kernel,
        grid=(t // bm,),
        in_specs=[row, _mod_spec(stream, bm, d, j_shift), _mod_spec(stream, bm, d, j_shift + 1)],
        out_specs=row,
        out_shape=jax.ShapeDtypeStruct((t, d), BF16),
        compiler_params=_params("parallel"),
        name="modulate",
    )(x, mod, mod)


def _resnorm_kernel(*refs, alpha, with_mod):
    if with_mod:
        x_ref, y_ref, gate_ref, g_ref, b_ref, sh_ref, sc_ref, xo_ref, h_ref = refs
    else:
        x_ref, y_ref, gate_ref, g_ref, b_ref, xo_ref = refs
    r = alpha * x_ref[...] + gate_ref[...] * y_ref[...]
    xn = _standardize(r) * g_ref[...] + b_ref[...]
    xo_ref[...] = xn
    if with_mod:
        h_ref[...] = (_standardize(xn) * (1.0 + sc_ref[...]) + sh_ref[...]).astype(BF16)


def _resnorm(x, y, mod, stream, j_gate, ln_g, ln_b, alpha, next_mod=None, j_next=None):
    t, d = x.shape
    bm = stream.row_block(t, 256)
    row = pl.BlockSpec((bm, d), lambda i: (i, 0))
    vec = pl.BlockSpec((1, d), lambda i: (0, 0))
    with_mod = next_mod is not None
    in_specs = [row, row, _mod_spec(stream, bm, d, j_gate), vec, vec]
    args = [x, y, mod, ln_g.reshape(1, d), ln_b.reshape(1, d)]
    out_specs = [row]
    out_shape = [jax.ShapeDtypeStruct((t, d), F32)]
    if with_mod:
        in_specs += [_mod_spec(stream, bm, d, j_next), _mod_spec(stream, bm, d, j_next + 1)]
        args += [next_mod, next_mod]
        out_specs.append(row)
        out_shape.append(jax.ShapeDtypeStruct((t, d), BF16))
    out = pl.pallas_call(
        functools.partial(_resnorm_kernel, alpha=alpha, with_mod=with_mod),
        grid=(t // bm,),
        in_specs=in_specs,
        out_specs=out_specs,
        out_shape=out_shape,
        compiler_params=_params("parallel"),
        name="resnorm",
    )(*args)
    return (out[0], out[1]) if with_mod else (out[0], None)


def _gelu_tanh(x):
    return x * (0.5 * (1.0 + jnp.tanh(math.sqrt(2.0 / math.pi) * (x + 0.044715 * (x * x * x)))))


def _mm_kernel(*refs, n_pairs, gelu):
    o_ref = refs[2 * n_pairs]
    acc = _dot(refs[0][...], refs[n_pairs][...])
    for p in range(1, n_pairs):
        acc = acc + _dot(refs[p][...], refs[n_pairs + p][...])
    if gelu:
        acc = _gelu_tanh(acc)
    o_ref[...] = acc.astype(o_ref.dtype)


def _mm(pairs, bm, bn, out_dtype=F32, gelu=False, name="mm"):
    pairs = [p if len(p) == 3 else (*p, 0) for p in pairs]
    m = pairs[0][0].shape[0]
    n = pairs[0][1].shape[1]
    bm, bn = _pick(m, bm), _pick(n, bn)
    assert all(w.shape[0] % x.shape[1] == 0 for x, w, _ in pairs)
    x_specs = [pl.BlockSpec((bm, x.shape[1]), lambda i, j: (i, 0)) for x, _, _ in pairs]
    w_specs = [pl.BlockSpec((x.shape[1], bn), lambda i, j, kb=kb: (kb, j)) for x, _, kb in pairs]
    return pl.pallas_call(
        functools.partial(_mm_kernel, n_pairs=len(pairs), gelu=gelu),
        grid=(m // bm, n // bn),
        in_specs=x_specs + w_specs,
        out_specs=pl.BlockSpec((bm, bn), lambda i, j: (i, j)),
        out_shape=jax.ShapeDtypeStruct((m, n), out_dtype),
        compiler_params=_params("parallel", "parallel"),
        name=name,
    )(*[x for x, _, _ in pairs], *[w for _, w, _ in pairs])


def _head_rms(z, gain):
    return z * lax.rsqrt(jnp.mean(z * z, axis=-1, keepdims=True) + LN_EPS) * gain


def _rope(z, c_ref, sa_ref, sb_ref):
    hd = z.shape[-1]
    return (z * c_ref[...] + pltpu.roll(z, hd - hd // 4, axis=1) * sa_ref[...]
            + pltpu.roll(z, hd // 4, axis=1) * sb_ref[...])


def _even_in_kernel(*refs, heads_per_block, nb, rope, state, q_scale):
    k_lo, v_lo = (1 + Q_PER_KV) * nb, (2 + Q_PER_KV) * nb
    h_ref, w_ref, qg_ref, kg_ref = refs[:4]
    refs = refs[4:]
    if rope:
        c_ref, sa_ref, sb_ref = refs[:3]
        refs = refs[3:]
    z_ref = refs[0]
    if state:
        k32_ref, v32_ref = refs[1:3]
    j = pl.program_id(1)
    z = _dot(h_ref[...], w_ref[...])

    def heads(gain_ref, scale, k32=None):
        for hh in range(heads_per_block):
            sl = slice(hh * HEAD_DIM, (hh + 1) * HEAD_DIM)
            n = _head_rms(z[:, sl], gain_ref[...])
            if k32 is not None:
                k32[:, sl] = n
            if rope:
                n = _rope(n, c_ref, sa_ref, sb_ref)
            if scale != 1.0:
                n = n * scale
            z_ref[:, sl] = n.astype(BF16)

    @pl.when(jnp.logical_or(j < nb, j >= v_lo))
    def _():
        z_ref[...] = z.astype(BF16)

    @pl.when(jnp.logical_and(j >= nb, j < k_lo))
    def _():
        heads(qg_ref, q_scale)

    @pl.when(jnp.logical_and(j >= k_lo, j < v_lo))
    def _():
        heads(kg_ref, 1.0, k32_ref if state else None)

    if state:
        @pl.when(j >= v_lo)
        def _():
            v32_ref[...] = z


def _even_in(h, w_in, q_g, k_g, stream, rope_tabs=None, state=False):
    t, d = h.shape
    n = w_in.shape[1]
    quarter = d // 4
    assert n == (3 + Q_PER_KV) * quarter
    bn = _pick(quarter, EVEN_IN_COLS)
    nb = quarter // bn
    k_lo, v_lo = (1 + Q_PER_KV) * nb, (2 + Q_PER_KV) * nb
    bm = stream.row_block(t, 1024)
    vec = pl.BlockSpec((1, HEAD_DIM), lambda i, j: (0, 0))
    in_specs = [pl.BlockSpec((bm, d), lambda i, j: (i, 0)),
                pl.BlockSpec((d, bn), lambda i, j: (0, j)), vec, vec]
    args = [h, w_in, q_g.reshape(1, HEAD_DIM), k_g.reshape(1, HEAD_DIM)]
    if rope_tabs is not None:
        blocks_per_seq = stream.seq // bm
        tab = pl.BlockSpec((bm, HEAD_DIM), lambda i, j: (i % blocks_per_seq, 0))
        in_specs += [tab, tab, tab]
        args += list(rope_tabs)
    out_specs = [pl.BlockSpec((bm, bn), lambda i, j: (i, j))]
    out_shape = [jax.ShapeDtypeStruct((t, n), BF16)]
    if state:
        out_specs += [pl.BlockSpec((bm, bn), lambda i, j: (i, jnp.clip(j - k_lo, 0, nb - 1))),
                      pl.BlockSpec((bm, bn), lambda i, j: (i, jnp.clip(j - v_lo, 0, nb - 1)))]
        out_shape += [jax.ShapeDtypeStruct((t, quarter), F32)] * 2
    return pl.pallas_call(
        functools.partial(_even_in_kernel, heads_per_block=bn // HEAD_DIM, nb=nb, rope=rope_tabs is not None,
                          state=state, q_scale=HEAD_DIM ** -0.5),
        grid=(t // bm, n // bn),
        in_specs=in_specs,
        out_specs=out_specs,
        out_shape=out_shape,
        compiler_params=_params("parallel", "arbitrary"),
        name="even_in",
    )(*args)


def _rope_tables(seq):
    rows = (jnp.arange(seq) // GRID_W).astype(F32)
    cols = (jnp.arange(seq) % GRID_W).astype(F32)
    nf = HEAD_DIM // 4
    inv = ROPE_BASE ** (-jnp.arange(nf, dtype=F32) / nf)
    lane = jnp.arange(HEAD_DIM)
    pos = jnp.where((lane // (2 * nf))[None, :] == 0, rows[:, None], cols[:, None])
    ang = pos * inv[lane % nf][None, :]
    first = ((lane // nf) % 2 == 0)[None, :]
    c, s = jnp.cos(ang), jnp.sin(ang)
    return c, jnp.where(first, -s, 0.0), jnp.where(first, 0.0, s)


def _attn_kernel(*refs, cached, bq):
    q_refs = refs[:Q_PER_KV]
    k_ref, v_ref = refs[Q_PER_KV:Q_PER_KV + 2]
    if cached:
        ck_ref, cv_ref = refs[Q_PER_KV + 2:Q_PER_KV + 4]
    o_ref = refs[-1]
    dims = (((1,), (1,)), ((), ()))
    def with_ones(v):
        lane = lax.broadcasted_iota(jnp.int32, (v.shape[0], HEAD_DIM), 1)
        return jnp.concatenate([v, (lane == 0).astype(BF16)], axis=1)

    half = k_ref.shape[0] // 2
    ks = [k_ref[0:half], k_ref[half:]]
    vs = [with_ones(v_ref[0:half]), with_ones(v_ref[half:])]
    if cached:
        ks.append(ck_ref[...].astype(BF16))
        vs.append(with_ones(cv_ref[...].astype(BF16)))
    chunk = min(bq, ATTN_CHAIN_ROWS)
    for t in range(Q_PER_KV):
        for c in range(bq // chunk):
            rows = slice(c * chunk, (c + 1) * chunk)
            q = q_refs[t][rows, :]
            ss = [lax.dot_general(q, kk, dims, preferred_element_type=F32) for kk in ks]
            m = functools.reduce(jnp.maximum, [jnp.max(s, axis=-1, keepdims=True) for s in ss])
            acc = sum(_dot(jnp.exp((s - m).astype(BF16)), vv) for s, vv in zip(ss, vs))
            o = acc[:, :HEAD_DIM] / acc[:, HEAD_DIM:HEAD_DIM + 1]
            o_ref[rows, t * HEAD_DIM:(t + 1) * HEAD_DIM] = o.astype(BF16)


def _attention(z, batch, seq, d, cache_k=None, cache_v=None):
    a_blocks = (d // 4) // HEAD_DIM
    n_kv = (d // 4) // HEAD_DIM
    n_q = Q_PER_KV * n_kv
    bq = _pick(seq, 1024)
    nqb = seq // bq
    cached = cache_k is not None

    def q_spec(t):
        return pl.BlockSpec((bq, HEAD_DIM), lambda b, g, r: (b * nqb + r, a_blocks + Q_PER_KV * g + t))

    in_specs = [q_spec(t) for t in range(Q_PER_KV)]
    in_specs += [pl.BlockSpec((seq, HEAD_DIM), lambda b, g, r: (b, a_blocks + n_q + g)),
                 pl.BlockSpec((seq, HEAD_DIM), lambda b, g, r: (b, a_blocks + n_q + n_kv + g))]
    args = [z] * (Q_PER_KV + 2)
    if cached:
        past = cache_k.shape[1]
        cspec = pl.BlockSpec((None, past, HEAD_DIM), lambda b, g, r: (b, 0, g))
        in_specs += [cspec, cspec]
        args += [cache_k, cache_v]
    return pl.pallas_call(
        functools.partial(_attn_kernel, cached=cached, bq=bq),
        grid=(batch, n_kv, nqb),
        in_specs=in_specs,
        out_specs=pl.BlockSpec((bq, Q_PER_KV * HEAD_DIM), lambda b, g, r: (b * nqb + r, g)),
        out_shape=jax.ShapeDtypeStruct((batch * seq, n_q * HEAD_DIM), BF16),
        compiler_params=_params("parallel", "parallel", "arbitrary"),
        name="attention",
    )(*args)


def _dft_mats(n):
    m = jnp.arange(n, dtype=jnp.int32)

    def tables(k, period):
        ang = ((k[:, None] * m[None, :]) % period).astype(F32) * (2.0 * math.pi / period)
        return jnp.cos(ang), jnp.sin(ang)

    r = math.isqrt(n)
    if r * r != n:
        c, s = tables(m, n)
        return c.astype(BF16), s.astype(BF16)
    kr = jnp.arange(r, dtype=jnp.int32)
    c1, s1 = tables(kr, r)
    c2, s2 = tables(kr, n)
    c = c1[:, None, :] * c2[None, :, :] - s1[:, None, :] * s2[None, :, :]
    s = s1[:, None, :] * c2[None, :, :] + c1[:, None, :] * s2[None, :, :]
    return c.reshape(n, n).astype(BF16), s.reshape(n, n).astype(BF16)


def _chan_dft_kernel(a_ref, cd_ref, sd_ref, pq_ref, *, groups):
    aw = a_ref.shape[1]
    dg = aw // groups
    for g in range(groups):
        ag = a_ref[:, g * dg:(g + 1) * dg]
        pq_ref[:, g * dg:(g + 1) * dg] = _dot(ag, cd_ref[...]).astype(BF16)
        pq_ref[:, aw + g * dg:aw + (g + 1) * dg] = _dot(ag, sd_ref[...]).astype(BF16)


def _chan_dft(z, aw):
    t = z.shape[0]
    dg = aw // A_GROUPS
    bm = _pick(t, 1024)
    cd, sd = _dft_mats(dg)
    full = pl.BlockSpec((dg, dg), lambda i: (0, 0))
    return pl.pallas_call(
        functools.partial(_chan_dft_kernel, groups=A_GROUPS),
        grid=(t // bm,),
        in_specs=[pl.BlockSpec((bm, aw), lambda i: (i, 0)), full, full],
        out_specs=pl.BlockSpec((bm, 2 * aw), lambda i: (i, 0)),
        out_shape=jax.ShapeDtypeStruct((t, 2 * aw), BF16),
        compiler_params=_params("parallel"),
        name="chan_dft",
    )(z, cd, sd)


def _seq_dft_kernel(pq_ref, cl_ref, sl_ref, fw_ref, fb_ref, y_ref, *, groups, norm):
    aw = y_ref.shape[1]
    dg = aw // groups
    f = (_dot(cl_ref[...], pq_ref[:, :aw]) - _dot(sl_ref[...], pq_ref[:, aw:])) * norm
    fb = fb_ref[...]
    for g in range(groups):
        sl = slice(g * dg, (g + 1) * dg)
        y_ref[:, sl] = (_dot(f[:, sl].astype(BF16), fw_ref[g]) + fb[:, sl]).astype(BF16)


def _seq_dft(pq, batch, seq, fw, fb):
    aw = pq.shape[1] // 2
    dg = aw // A_GROUPS
    bl = _pick(seq, 512)
    nlb = seq // bl
    cl, sl = _dft_mats(seq)
    mat = pl.BlockSpec((bl, seq), lambda b, r: (r, 0))
    return pl.pallas_call(
        functools.partial(_seq_dft_kernel, groups=A_GROUPS, norm=1.0 / math.sqrt(seq * dg)),
        grid=(batch, nlb),
        in_specs=[pl.BlockSpec((seq, 2 * aw), lambda b, r: (b, 0), pipeline_mode=pl.Buffered(1)),
                  mat, mat,
                  pl.BlockSpec((A_GROUPS, dg, dg), lambda b, r: (0, 0, 0)),
                  pl.BlockSpec((1, aw), lambda b, r: (0, 0))],
        out_specs=pl.BlockSpec((bl, aw), lambda b, r: (b * nlb + r, 0)),
        out_shape=jax.ShapeDtypeStruct((batch * seq, aw), BF16),
        compiler_params=_params("parallel", "arbitrary"),
        name="seq_dft",
    )(pq, cl, sl, fw, fb.reshape(1, aw))


def _ffn_up_kernel(hp_ref, h_ref, hn_ref, wg1_ref, wg2_ref, wv1_ref, wv2_ref, cg1_ref, cg2_ref, cv1_ref, cv2_ref,
                   bg1_ref, bg2_ref, bv1_ref, bv2_ref, pcg_ref, pcv_ref, pbg_ref, pbv_ref,
                   ue_ref, uo_ref, ul_ref, hh_ref, zg_ref, zv_ref, *, bm, seq, ni, nj):
    s = pl.program_id(0)
    i = s // nj
    halo = CONV_HALO
    rows = bm + 2 * halo
    interior = seq < bm

    @pl.when(s == 0)
    def _():
        zg_ref[...] = jnp.zeros_like(zg_ref)
        zv_ref[...] = jnp.zeros_like(zv_ref)

    @pl.when(s % nj == 0)
    def _():
        lo_ok = (i * bm) % seq != 0
        hi_ok = ((i + 1) * bm) % seq != 0
        hh_ref[0:halo] = jnp.where(lo_ok, hp_ref[...], jnp.zeros_like(hp_ref))
        hh_ref[halo:halo + bm] = h_ref[...]
        hh_ref[halo + bm:rows] = jnp.where(hi_ok, hn_ref[...], jnp.zeros_like(hn_ref))

    hh = hh_ref[...]
    if interior:
        pos = lax.broadcasted_iota(jnp.int32, (bm, 1), 0) % seq
        not_first = (pos != 0).astype(F32)
        not_last = (pos != seq - 1).astype(F32)

    def conv(z, cw_ref, cb_ref):
        prev = pltpu.roll(z, 1, axis=0)[halo:halo + bm]
        nxt = pltpu.roll(z, rows - 1, axis=0)[halo:halo + bm]
        if interior:
            prev, nxt = prev * not_first, nxt * not_last
        return prev * cw_ref[0:1] + z[halo:halo + bm] * cw_ref[1:2] + nxt * cw_ref[2:3] + cb_ref[...]

    def gate(zg, zv, cwg, cwv, cbg, cbv):
        g = conv(zg, cwg, cbg)
        v = conv(zv, cwv, cbv)
        return (g * jax.nn.sigmoid(g) * v).astype(BF16)

    uo_ref[...] = gate(zg_ref[...], zv_ref[...], pcg_ref, pcv_ref, pbg_ref, pbv_ref)
    ue_ref[...] = gate(_dot(hh, wg1_ref[...]), _dot(hh, wv1_ref[...]), cg1_ref, cv1_ref, bg1_ref, bv1_ref)
    zg_ref[...] = _dot(hh, wg2_ref[...])
    zv_ref[...] = _dot(hh, wv2_ref[...])

    @pl.when(s == ni * nj - 1)
    def _():
        ul_ref[...] = gate(zg_ref[...], zv_ref[...], cg2_ref, cv2_ref, bg2_ref, bv2_ref)


def _ffn_up(h, w_up, conv_w, conv_b, stream, layer):
    t, d = h.shape
    f = w_up.shape[1] // 2
    bm = stream.row_block(t, 1024)
    sub = FFN_COLS
    ni, nj = t // bm, f // (2 * sub)
    halo = CONV_HALO
    hb = bm // halo
    last = t // halo - 1
    assert f % (2 * sub) == 0 and bm % halo == 0 and conv_w.shape[0] == 3

    def row(s):
        return s // nj

    def prev(s):
        return jnp.maximum(s - 1, 0)

    def cur(rows, off, row_block=0):
        return pl.BlockSpec((rows, sub), lambda s: (row_block, s % nj + off * nj))

    def old(rows, off):
        return pl.BlockSpec((rows, sub), lambda s: (0, prev(s) % nj + off * nj))

    half = jax.ShapeDtypeStruct((t, f // 2), BF16)
    u_first, u_second, u_last = pl.pallas_call(
        functools.partial(_ffn_up_kernel, bm=bm, seq=stream.seq, ni=ni, nj=nj),
        grid=(ni * nj,),
        in_specs=[pl.BlockSpec((halo, d), lambda s: (jnp.maximum(row(s) * hb - 1, 0), 0)),
                  pl.BlockSpec((bm, d), lambda s: (row(s), 0), pipeline_mode=pl.Buffered(1)),
                  pl.BlockSpec((halo, d), lambda s: (jnp.minimum((row(s) + 1) * hb, last), 0)),
                  cur(d, 0, layer), cur(d, 1, layer), cur(d, 2, layer), cur(d, 3, layer),
                  cur(3, 0), cur(3, 1), cur(3, 2), cur(3, 3),
                  cur(1, 0), cur(1, 1), cur(1, 2), cur(1, 3),
                  old(3, 1), old(3, 3), old(1, 1), old(1, 3)],
        out_specs=[pl.BlockSpec((bm, sub), lambda s: (row(s), s % nj)),
                   pl.BlockSpec((bm, sub), lambda s: (prev(s) // nj, prev(s) % nj)),
                   pl.BlockSpec((bm, sub), lambda s: (0, 0))],
        out_shape=[half, half, jax.ShapeDtypeStruct((bm, sub), BF16)],
        scratch_shapes=[pltpu.VMEM((bm + 2 * halo, d), BF16),
                        pltpu.VMEM((bm + 2 * halo, sub), F32),
                        pltpu.VMEM((bm + 2 * halo, sub), F32)],
        compiler_params=_params("arbitrary"),
        name="ffn_up",
    )(h, h, h, *[w_up] * 4, *[conv_w] * 4, *[conv_b] * 4, conv_w, conv_w, conv_b, conv_b)
    return u_first, u_second.at[(ni - 1) * bm:, (nj - 1) * sub:].set(u_last)


def _pad_ffn(w_up, conv_w, conv_b, w_down):
    f = w_down.shape[-2]
    pad = -f % (2 * FFN_COLS)

    def halves(a, dtype):
        a = a.astype(dtype)
        zeros = jnp.zeros((a.shape[0], pad), dtype)
        return jnp.concatenate([a[:, :f], zeros, a[:, f:], zeros], axis=1)

    w_down = jnp.pad(w_down.astype(BF16), ((0, 0),) * (w_down.ndim - 2) + ((0, pad), (0, 0)))
    return (halves(w_up, BF16), halves(conv_w, F32), halves(conv_b.reshape(-1, 2 * f), F32),
            w_down.reshape(-1, w_down.shape[-1]))


def _gmlp_mix_kernel(u_ref, v_ref, g_ref, b_ref, ws_ref, bst_ref, o_ref, *, groups):
    bm, c = v_ref.shape
    cg = c // groups
    vn = (_standardize(v_ref[...].astype(F32)) * g_ref[...] + b_ref[...]).astype(BF16)
    for ch in range(bm // CHUNK):
        rows = slice(ch * CHUNK, (ch + 1) * CHUNK)
        for g in range(groups):
            cols = slice(g * cg, (g + 1) * cg)
            mixed = _dot(ws_ref[g], vn[rows, cols]) + bst_ref[:, g:g + 1]
            o_ref[rows, cols] = (u_ref[rows, cols].astype(F32) * mixed).astype(BF16)


def _gmlp_mix(uv, v_g, v_b, ws, bs, stream):
    t, c2 = uv.shape
    c = c2 // 2
    bm = _pick(stream.seq, 256)
    assert bm % CHUNK == 0
    vec = pl.BlockSpec((1, c), lambda i: (0, 0))
    return pl.pallas_call(
        functools.partial(_gmlp_mix_kernel, groups=C_GROUPS),
        grid=(t // bm,),
        in_specs=[pl.BlockSpec((bm, c), lambda i: (i, 0)),
                  pl.BlockSpec((bm, c), lambda i: (i, 1)),
                  vec, vec,
                  pl.BlockSpec((C_GROUPS, CHUNK, CHUNK), lambda i: (0, 0, 0)),
                  pl.BlockSpec((CHUNK, C_GROUPS), lambda i: (0, 0))],
        out_specs=pl.BlockSpec((bm, c), lambda i: (i, 0)),
        out_shape=jax.ShapeDtypeStruct((t, c), BF16),
        compiler_params=_params("parallel"),
        name="gmlp_mix",
    )(uv, uv, v_g.reshape(1, c), v_b.reshape(1, c), ws, bs.T)


def kernel(x_prompt, x_sample, cache_k, cache_v, c, c_ctx, w_mod, b_mod, ln_g, ln_b, attn_w_in, fourier_w, fourier_b, q_norm_g, k_norm_g, attn_w_out, gmlp_w_in, gmlp_v_ln_g, gmlp_v_ln_b, gmlp_ws, gmlp_bs, gmlp_w_out, ffn_w_up, ffn_conv_w, ffn_conv_b, ffn_w_down):
    batch, seq, d = x_prompt.shape
    dec_batch, dec_seq, _ = x_sample.shape
    depth = w_mod.shape[0]
    n_kv = (d // 4) // HEAD_DIM
    aw = d // 4
    alpha = (2 * depth) ** 0.25
    assert dec_batch + 1 <= COND_ROWS and dec_seq % GRID_W == 0

    streams = (_Stream(seq, 0, False), _Stream(dec_seq, 1, True))
    xs = [x_prompt.reshape(batch * seq, d), x_sample.reshape(dec_batch * dec_seq, d)]
    batches = (batch, dec_batch)

    cond = jnp.zeros((COND_ROWS, d), F32).at[0].set(c_ctx).at[1:1 + dec_batch].set(c)
    mods = _adaln(cond, w_mod, b_mod).reshape(depth, COND_ROWS * N_MOD, 1, d)
    rope_tabs = _rope_tables(dec_seq)
    past = cache_k.shape[2]
    new_k, new_v = [], []

    w_up_all, conv_w_all, conv_b_all, w_down_all = _pad_ffn(
        ffn_w_up.reshape(depth * d, -1), ffn_conv_w.reshape(depth * 3, -1), ffn_conv_b, ffn_w_down)
    hs = [_modulate(xs[s], mods[0], streams[s], 0) for s in range(2)]
    for l in range(depth):
        i = l // 2
        mod = mods[l]
        ys = []
        if l % 2 == 0:
            w_in = attn_w_in[i].astype(BF16)
            w_out_a = attn_w_out[i, :aw].astype(BF16)
            w_out_o = attn_w_out[i, aw:].astype(BF16)
            fw = fourier_w[i].astype(BF16)
            fb = fourier_b[i].reshape(aw)
            ck = cache_k[:, i].reshape(dec_batch, past, n_kv * HEAD_DIM)
            cv = cache_v[:, i].reshape(dec_batch, past, n_kv * HEAD_DIM)
            for s in range(2):
                st = streams[s]
                if s == 0:
                    z, k32, v32 = _even_in(hs[s], w_in, q_norm_g[i], k_norm_g[i], st, state=True)
                    new_k.append(k32.reshape(batch, seq, n_kv, HEAD_DIM))
                    new_v.append(v32.reshape(batch, seq, n_kv, HEAD_DIM))
                    o = _attention(z, batch, seq, d)
                else:
                    (z,) = _even_in(hs[s], w_in, q_norm_g[i], k_norm_g[i], st, rope_tabs=rope_tabs)
                    o = _attention(z, dec_batch, dec_seq, d, ck, cv)
                ya = _seq_dft(_chan_dft(z, aw), batches[s], st.seq, fw, fb)
                ys.append(_mm([(ya, w_out_a), (o, w_out_o)], 1024, 1024, name="even_out"))
        else:
            w_in = gmlp_w_in[i].astype(BF16)
            w_out = gmlp_w_out[i].astype(BF16)
            ws = gmlp_ws[i].astype(BF16)
            for s in range(2):
                uv = _mm([(hs[s], w_in)], 1024, 1024, out_dtype=BF16, gelu=True, name="gmlp_in")
                gated = _gmlp_mix(uv, gmlp_v_ln_g[i], gmlp_v_ln_b[i], ws, gmlp_bs[i], streams[s])
                ys.append(_mm([(gated, w_out)], 1024, 1024, name="gmlp_out"))

        conv_w, conv_b = conv_w_all[3 * l:3 * l + 3], conv_b_all[l:l + 1]
        for s in range(2):
            st = streams[s]
            xs[s], h2 = _resnorm(xs[s], ys[s], mod, st, 2, ln_g[l, 0], ln_b[l, 0], alpha, mod, 3)
            u_first, u_second = _ffn_up(h2, w_up_all, conv_w, conv_b, st, l)
            f = _mm([(u_first, w_down_all, 2 * l), (u_second, w_down_all, 2 * l + 1)], 512, 512, name="ffn_down")
            nxt = mods[l + 1] if l + 1 < depth else None
            xs[s], hs[s] = _resnorm(xs[s], f, mod, st, 5, ln_g[l, 1], ln_b[l, 1], alpha, nxt, 0)

    state_k = jnp.stack(new_k, axis=1)
    state_v = jnp.stack(new_v, axis=1)
    return (xs[0].reshape(batch, seq, d), xs[1].reshape(dec_batch, dec_seq, d), state_k, state_v)
```
